```python
import math
import jax, jax.numpy as jnp
from jax import lax
import numpy as np

D_MODEL = 1024
BATCH = 32
SEQ = 2048
DEPTH = 4
DEC_BATCH = 1
DEC_SEQ = 16384
PAST_LEN = 128

N_MIXERS = 3
N_LAYERS_A = (DEPTH + 2) // 3
N_LAYERS_B = (DEPTH + 1) // 3
N_LAYERS_C = DEPTH // 3
EPS = 1e-6

A_HEADS = 4
A_DQK = D_MODEL // (2 * A_HEADS)
A_DV = D_MODEL // A_HEADS
A_CHUNK = 64
A_IN = 2 * A_HEADS * A_DQK + 2 * A_HEADS * A_DV + 4 * A_HEADS

B_HEADS = 8
B_DK = D_MODEL // B_HEADS
B_DV = D_MODEL // B_HEADS
B_CONV = 5
B_CHUNK = 64
B_IN = 2 * B_HEADS * B_DK + 2 * B_HEADS * B_DV + 4 * B_HEADS

C_HEADS = 8
C_DH = D_MODEL // (2 * C_HEADS)
C_DV = 2 * C_DH
C_IN = 2 * C_HEADS * 2 * C_DH + C_HEADS * C_DV
Q_BLOCK = 128
REL_BUCKETS = 32
REL_MAX_DIST = 128

D_FF = 2816
FFN_CONV = 3

kernel_name = "hybrid_bidir_mlstm_gdn_diffattn_encoder"


def _rms(x):
    xf = x.astype(jnp.float32)
    return xf * lax.rsqrt(jnp.mean(xf * xf, -1, keepdims=True) + EPS)


def _l2n(x):
    return x * lax.rsqrt(jnp.sum(x * x, -1, keepdims=True) + EPS)


def _dwconv(x, w):
    k, ch = w.shape
    return lax.conv_general_dilated(x, w[:, None, :].astype(x.dtype), (1,), [(k // 2, k // 2)],
                                    dimension_numbers=("NWC", "WIO", "NWC"),
                                    feature_group_count=ch)


def _heads(t, n):
    b, s, _ = t.shape
    return t.reshape(b, s, n, -1).transpose(0, 2, 1, 3).astype(jnp.float32)


def _merge(t):
    b, n, s, d = t.shape
    return t.transpose(0, 2, 1, 3).reshape(b, s, n * d)


def _flip(t):
    return jnp.flip(t, 2)


def _mlstm_scan(q, k, v, ig, lf):
    b, nh, s, dqk = q.shape
    dv = v.shape[-1]
    L = A_CHUNK
    nc = s // L

    def chunks(t):
        return jnp.moveaxis(t.reshape(b, nh, nc, L, *t.shape[3:]), 2, 0)

    xs = tuple(chunks(t) for t in (q, k, v, ig, lf))
    tri = jnp.tril(jnp.ones((L, L), bool))

    def step(carry, inp):
        C, n, m = carry
        qj, kj, vj, ij, fj = inp
        bc = jnp.cumsum(fj, -1)
        dmat = jnp.where(tri, bc[..., :, None] - bc[..., None, :] + ij[..., None, :], -jnp.inf)
        inter = bc + m[..., None]
        mj = jnp.maximum(inter, jnp.max(dmat, -1))
        w_intra = jnp.exp(dmat - mj[..., None])
        w_inter = jnp.exp(inter - mj)
        sqk = jnp.einsum("bhtd,bhsd->bhts", qj, kj) * w_intra
        num = (w_inter[..., None] * jnp.einsum("bhtd,bhde->bhte", qj, C)
               + jnp.einsum("bhts,bhse->bhte", sqk, vj))
        den = w_inter * jnp.einsum("bhtd,bhd->bht", qj, n) + jnp.sum(sqk, -1)
        h = num / jnp.maximum(jnp.abs(den), jnp.exp(-mj))[..., None]
        blast = bc[..., -1]
        dlast = blast[..., None] - bc + ij
        m_new = jnp.maximum(blast + m, jnp.max(dlast, -1))
        w_s = jnp.exp(dlast - m_new[..., None])
        decay = jnp.exp(blast + m - m_new)
        C_new = decay[..., None, None] * C + jnp.einsum("bhs,bhsd,bhse->bhde", w_s, kj, vj)
        n_new = decay[..., None] * n + jnp.einsum("bhs,bhsd->bhd", w_s, kj)
        return (C_new, n_new, m_new), h

    init = (jnp.zeros((b, nh, dqk, dv), jnp.float32), jnp.zeros((b, nh, dqk), jnp.float32),
            jnp.zeros((b, nh), jnp.float32))
    _, h = lax.scan(step, init, xs)
    return jnp.moveaxis(h, 0, 2).reshape(b, nh, s, dv)


def _gdn_scan(q, k, v, g, beta):
    b, nh, s, dk = q.shape
    dv = v.shape[-1]
    L = B_CHUNK
    nc = s // L

    def chunks(t):
        return t.reshape(b, nh, nc, L, *t.shape[3:])

    q, k, v, g, beta = (chunks(t) for t in (q, k, v, g, beta))
    gc = jnp.cumsum(g, -1)
    causal = jnp.tril(jnp.ones((L, L), bool))
    strict = causal & ~jnp.eye(L, dtype=bool)
    decay = jnp.exp(jnp.where(causal, gc[..., :, None] - gc[..., None, :], -jnp.inf))
    kb = k * beta[..., None]
    a_strict = jnp.where(strict, jnp.einsum("bhnid,bhnjd->bhnij", kb, k) * decay, 0.0)
    rhs = jnp.concatenate([v * beta[..., None], kb * jnp.exp(gc)[..., None]], -1)
    sol = lax.linalg.triangular_solve(a_strict + jnp.eye(L, dtype=a_strict.dtype), rhs,
                                      left_side=True, lower=True, unit_diagonal=True)
    u, w = sol[..., :dv], sol[..., dv:]
    qk = jnp.einsum("bhnid,bhnjd->bhnij", q, k) * decay
    qg = q * jnp.exp(gc)[..., None]
    kd = k * jnp.exp(gc[..., -1:] - gc)[..., None]
    gl = jnp.exp(gc[..., -1])
    xs = tuple(jnp.moveaxis(t, 2, 0) for t in (u, w, qk, qg, kd, gl))

    def step(state, inp):
        u_i, w_i, qk_i, qg_i, kd_i, gl_i = inp
        v_new = u_i - jnp.einsum("bhld,bhde->bhle", w_i, state)
        o = jnp.einsum("bhld,bhde->bhle", qg_i, state) + jnp.einsum("bhls,bhse->bhle", qk_i, v_new)
        state = state * gl_i[..., None, None] + jnp.einsum("bhld,bhle->bhde", kd_i, v_new)
        return state, o

    _, o = lax.scan(step, jnp.zeros((b, nh, dk, dv), jnp.float32), xs)
    return jnp.moveaxis(o, 0, 2).reshape(b, nh, s, dv)


def _mlstm_mixer(h, w_in, gate_b, norm_g, w_out):
    b, s, _ = h.shape
    nq, nv = A_HEADS * A_DQK, A_HEADS * A_DV
    proj = h @ w_in
    q, k, v, o, gates = jnp.split(proj, [nq, 2 * nq, 2 * nq + nv, 2 * nq + 2 * nv], -1)
    q = _heads(q, A_HEADS) * (A_DQK ** -0.5)
    k = _heads(k, A_HEADS)
    v = _heads(v, A_HEADS)
    gates = (gates.astype(jnp.float32) + gate_b.astype(jnp.float32)).reshape(b, s, 4, A_HEADS)
    gates = gates.transpose(2, 0, 3, 1)
    h_f = _mlstm_scan(q, k, v, gates[0], jax.nn.log_sigmoid(gates[1]))
    h_b = _mlstm_scan(_flip(q), _flip(k), _flip(v), _flip(gates[2]),
                      jax.nn.log_sigmoid(_flip(gates[3])))
    hs = _rms(h_f + _flip(h_b)) * norm_g.astype(jnp.float32).reshape(A_HEADS, 1, A_DV)
    out = jax.nn.sigmoid(o.astype(jnp.float32)) * _merge(hs)
    return out.astype(h.dtype) @ w_out


def _gdn_mixer(h, w_in, conv_w, a_log, dt_bias, norm_g, w_out):
    b, s, _ = h.shape
    nk, nv = B_HEADS * B_DK, B_HEADS * B_DV
    proj = h @ w_in
    qkv, z, gates = jnp.split(proj, [2 * nk + nv, 2 * nk + 2 * nv], -1)
    qkv = jax.nn.silu(_dwconv(qkv, conv_w))
    q, k, v = jnp.split(qkv, [nk, 2 * nk], -1)
    q = _l2n(_heads(q, B_HEADS)) * (B_DK ** -0.5)
    k = _l2n(_heads(k, B_HEADS))
    v = _heads(v, B_HEADS)
    gates = gates.astype(jnp.float32).reshape(b, s, 2, 2, B_HEADS)
    g = -jnp.exp(a_log.astype(jnp.float32)) * jax.nn.softplus(gates[:, :, :, 0] + dt_bias.astype(jnp.float32))
    beta = jax.nn.sigmoid(gates[:, :, :, 1])
    g = g.transpose(2, 0, 3, 1)
    beta = beta.transpose(2, 0, 3, 1)
    o_f = _gdn_scan(q, k, v, g[0], beta[0])
    o_b = _gdn_scan(_flip(q), _flip(k), _flip(v), _flip(g[1]), _flip(beta[1]))
    o = _rms(o_f + _flip(o_b)) * norm_g.astype(jnp.float32)
    out = _merge(o) * jax.nn.silu(z.astype(jnp.float32))
    return out.astype(h.dtype) @ w_out


def _rel_bucket(rel):
    nb = REL_BUCKETS // 2
    exact = nb // 2
    n = jnp.abs(rel)
    large = exact + (jnp.log(jnp.maximum(n, 1).astype(jnp.float32) / exact)
                     / math.log(REL_MAX_DIST / exact) * (nb - exact)).astype(jnp.int32)
    large = jnp.minimum(large, nb - 1)
    return jnp.where(rel > 0, nb, 0) + jnp.where(n < exact, n, large)


def _diff_attn_mixer(h, w_in, lam_p, norm_g, w_out, rel_bias, lambda_init):
    b, s, _ = h.shape
    nqk = C_HEADS * 2 * C_DH
    proj = h @ w_in
    q, k, v = jnp.split(proj, [nqk, 2 * nqk], -1)
    q = q.reshape(b, s, C_HEADS, 2, C_DH).transpose(3, 0, 2, 1, 4)
    k = k.reshape(b, s, C_HEADS, 2, C_DH).transpose(3, 0, 2, 1, 4)
    v = v.reshape(b, s, C_HEADS, C_DV).transpose(0, 2, 1, 3)
    lp = lam_p.astype(jnp.float32)
    lam = jnp.exp(jnp.sum(lp[0] * lp[1])) - jnp.exp(jnp.sum(lp[2] * lp[3])) + lambda_init
    nblk = s // Q_BLOCK
    qb = jnp.moveaxis(q.reshape(2, b, C_HEADS, nblk, Q_BLOCK, C_DH), 3, 0)
    starts = jnp.arange(nblk, dtype=jnp.int32) * Q_BLOCK
    kpos = jnp.arange(s, dtype=jnp.int32)
    table = rel_bias.astype(jnp.float32)
    scale = C_DH ** -0.5

    def block(args):
        q_blk, start = args
        qpos = start + jnp.arange(Q_BLOCK, dtype=jnp.int32)
        bias = jnp.transpose(table[_rel_bucket(kpos[None, :] - qpos[:, None])], (2, 0, 1))
        logits = jnp.einsum("mbhqd,mbhkd->mbhqk", q_blk, k).astype(jnp.float32) * scale + bias
        p = jax.nn.softmax(logits, -1)
        a = p[0] - lam * p[1]
        return jnp.einsum("bhqk,bhkd->bhqd", a.astype(v.dtype), v)

    o = lax.map(block, (qb, starts))
    o = jnp.moveaxis(o, 0, 2).reshape(b, C_HEADS, s, C_DV)
    o = _rms(o) * norm_g.astype(jnp.float32) * (1.0 - lambda_init)
    return _merge(o).astype(h.dtype) @ w_out


def _conv_ffn(h, w_up, conv_w, conv_b, w_down):
    u = _dwconv(h @ w_up, conv_w) + conv_b
    a, g = jnp.split(u, 2, -1)
    return (a * jax.nn.silu(g)) @ w_down


def _modulate(x, shift, scale):
    return (_rms(x) * (1.0 + scale.astype(jnp.float32)) + shift.astype(jnp.float32)).astype(x.dtype)


def _trunk(x, c, ada_w, ada_b, a_w_in, a_gate_b, a_norm_g, a_w_out, b_w_in, b_conv_w, b_a_log,
           b_dt_bias, b_norm_g, b_w_out, c_w_in, c_lambda, c_norm_g, c_w_out, rel_bias,
           ffn_w_up, ffn_conv_w, ffn_conv_b, ffn_w_down, final_g):
    cs = jax.nn.silu(c)
    for i in range(DEPTH):
        mod = (cs @ ada_w[i] + ada_b[i])[:, None, :]
        sh1, sc1, g1, sh2, sc2, g2 = jnp.split(mod, 6, -1)
        hn = _modulate(x, sh1, sc1)
        j = i // N_MIXERS
        kind = i % N_MIXERS
        if kind == 0:
            y = _mlstm_mixer(hn, a_w_in[j], a_gate_b[j], a_norm_g[j], a_w_out[j])
        elif kind == 1:
            y = _gdn_mixer(hn, b_w_in[j], b_conv_w[j], b_a_log[j], b_dt_bias[j], b_norm_g[j], b_w_out[j])
        else:
            y = _diff_attn_mixer(hn, c_w_in[j], c_lambda[j], c_norm_g[j], c_w_out[j], rel_bias,
                                 0.8 - 0.6 * math.exp(-0.3 * i))
        x = x + g1 * y
        hn = _modulate(x, sh2, sc2)
        x = x + g2 * _conv_ffn(hn, ffn_w_up[i], ffn_conv_w[i], ffn_conv_b[i], ffn_w_down[i])
    return (_rms(x) * final_g.astype(jnp.float32)).astype(x.dtype)


def setup_inputs(seed: int = 0) -> dict:
    key = jax.random.key(seed)
    ks = list(jax.random.split(key, 32))

    def nrm(i, shape, scale):
        return jax.random.normal(ks[i], shape, jnp.float32) * scale

    D = D_MODEL
    fb = jnp.linspace(3.0, 6.0, A_HEADS)
    ib = jnp.full((A_HEADS,), -1.0)
    gate_base = jnp.concatenate([ib, fb, ib, fb])
    dt = jnp.exp(jax.random.uniform(ks[13], (N_LAYERS_B, 2, B_HEADS), jnp.float32,
                                    math.log(1e-3), math.log(1e-1)))
    n_qkv = 2 * B_HEADS * B_DK + B_HEADS * B_DV
    return {
        "x_prompt": nrm(0, (BATCH, SEQ, D), 1.0),
        "x_sample": nrm(1, (DEC_BATCH, DEC_SEQ, D), 1.0),
        "c_prompt": nrm(2, (BATCH, D), 1.0),
        "c_sample": nrm(3, (DEC_BATCH, D), 1.0),
        "ada_w": nrm(4, (DEPTH, D, 6 * D), 0.5 * D ** -0.5),
        "ada_b": nrm(5, (DEPTH, 6 * D), 0.02),
        "a_w_in": nrm(6, (N_LAYERS_A, D, A_IN), D ** -0.5),
        "a_gate_b": gate_base[None, :] + nrm(7, (N_LAYERS_A, 4 * A_HEADS), 0.1),
        "a_norm_g": 1.0 + nrm(8, (N_LAYERS_A, A_HEADS * A_DV), 0.1),
        "a_w_out": nrm(9, (N_LAYERS_A, A_HEADS * A_DV, D), (A_HEADS * A_DV) ** -0.5),
        "b_w_in": nrm(10, (N_LAYERS_B, D, B_IN), D ** -0.5),
        "b_conv_w": nrm(11, (N_LAYERS_B, B_CONV, n_qkv), B_CONV ** -0.5),
        "b_a_log": jnp.log(jax.random.uniform(ks[12], (N_LAYERS_B, 2, B_HEADS), jnp.float32, 1.0, 16.0)),
        "b_dt_bias": dt + jnp.log(-jnp.expm1(-dt)),
        "b_norm_g": 1.0 + nrm(14, (N_LAYERS_B, B_DV), 0.1),
        "b_w_out": nrm(15, (N_LAYERS_B, B_HEADS * B_DV, D), (B_HEADS * B_DV) ** -0.5),
        "c_w_in": nrm(16, (N_LAYERS_C, D, C_IN), D ** -0.5),
        "c_lambda": nrm(17, (N_LAYERS_C, 4, C_DH), 0.1),
        "c_norm_g": 1.0 + nrm(18, (N_LAYERS_C, C_DV), 0.1),
        "c_w_out": nrm(19, (N_LAYERS_C, C_HEADS * C_DV, D), (C_HEADS * C_DV) ** -0.5),
        "rel_bias": nrm(20, (REL_BUCKETS, C_HEADS), 0.5),
        "ffn_w_up": nrm(21, (DEPTH, D, 2 * D_FF), D ** -0.5),
        "ffn_conv_w": nrm(22, (DEPTH, FFN_CONV, 2 * D_FF), FFN_CONV ** -0.5),
        "ffn_conv_b": nrm(23, (DEPTH, 2 * D_FF), 0.02),
        "ffn_w_down": nrm(24, (DEPTH, D_FF, D), D_FF ** -0.5),
        "final_g": 1.0 + nrm(25, (D,), 0.1),
    }


def reference(x_prompt, x_sample, c_prompt, c_sample, ada_w, ada_b, a_w_in, a_gate_b, a_norm_g,
              a_w_out, b_w_in, b_conv_w, b_a_log, b_dt_bias, b_norm_g, b_w_out, c_w_in, c_lambda,
              c_norm_g, c_w_out, rel_bias, ffn_w_up, ffn_conv_w, ffn_conv_b, ffn_w_down, final_g):
    weights = (ada_w, ada_b, a_w_in, a_gate_b, a_norm_g, a_w_out, b_w_in, b_conv_w, b_a_log,
               b_dt_bias, b_norm_g, b_w_out, c_w_in, c_lambda, c_norm_g, c_w_out, rel_bias,
               ffn_w_up, ffn_conv_w, ffn_conv_b, ffn_w_down, final_g)
    y_prompt = _trunk(x_prompt, c_prompt, *weights)
    y_sample = _trunk(x_sample, c_sample, *weights)
    return (y_prompt, y_sample)
```

```python
import functools
import math

import jax
import jax.numpy as jnp
from jax import lax
from jax.experimental import pallas as pl
from jax.experimental.pallas import tpu as pltpu

F32 = jnp.float32
BF16 = jnp.bfloat16
EPS = 1e-6

N_MIXERS = 3
A_HEADS = 4
B_HEADS = 8
B_CONV = 5
C_HEADS = 8
REL_BUCKETS = 32
REL_MAX_DIST = 128
FFN_CONV = 3

LANES = 128
SUBLANES = 8
HALO = SUBLANES
VMEM_LIMIT = 56 * 1024 * 1024

ROW_TILE = 512
SCAN_BLOCK = 512
SCAN_CHUNK = 64
ATTN_TILE = 512
FFN_COLS = 256


def _params(*sem):
    return pltpu.CompilerParams(dimension_semantics=sem, vmem_limit_bytes=VMEM_LIMIT)


def _sigmoid(x):
    return 1.0 / (1.0 + jnp.exp(-x))


def _softplus(x):
    return jnp.maximum(x, 0.0) + jnp.log(1.0 + jnp.exp(-jnp.abs(x)))


def _rms_rows(x):
    return x * lax.rsqrt(jnp.mean(x * x, axis=-1, keepdims=True) + EPS)


def _mm(a, b):
    return jnp.dot(a, b, preferred_element_type=F32)


def _mm_nt(a, b):
    return lax.dot_general(a, b, (((1,), (1,)), ((), ())), preferred_element_type=F32)


def _mm_tn(a, b):
    return lax.dot_general(a, b, (((0,), (0,)), ((), ())), preferred_element_type=F32)


def _mod_kernel(c_ref, w_ref, b_ref, o_ref):
    c = c_ref[...]
    cs = c * _sigmoid(c)
    o_ref[0] = _mm(cs.astype(BF16), w_ref[0].astype(BF16)) + b_ref[0]


def _ada_mod(c, ada_w, ada_b):
    depth, d, n = ada_w.shape
    bp = c.shape[0]
    tn = 1536
    return pl.pallas_call(
        _mod_kernel,
        out_shape=jax.ShapeDtypeStruct((depth, bp, n), F32),
        grid=(depth, n // tn),
        in_specs=[pl.BlockSpec((bp, d), lambda l, j: (0, 0)),
                  pl.BlockSpec((1, d, tn), lambda l, j: (l, 0, j)),
                  pl.BlockSpec((1, 1, tn), lambda l, j: (l, 0, j))],
        out_specs=pl.BlockSpec((1, bp, tn), lambda l, j: (l, 0, j)),
        compiler_params=_params("parallel", "parallel"),
        name="ada_mod",
    )(c, ada_w, ada_b.reshape(depth, 1, n))


def _inproj_kernel(x_ref, sh_ref, sc_ref, w_ref, *o_refs, n_main, tn):
    hn = (_rms_rows(x_ref[0]) * (1.0 + sc_ref[0]) + sh_ref[0]).astype(BF16)
    for c0 in range(0, n_main, tn):
        o_refs[0][0, :, c0:c0 + tn] = _mm(hn, w_ref[:, c0:c0 + tn])
    if len(o_refs) > 1:
        o_refs[1][0] = _mm(hn, w_ref[:, n_main:])


def _inproj(x, mod, w, n_main):
    b, s, d = x.shape
    n_all = w.shape[1]
    tm = min(ROW_TILE, s)
    tn = 512
    out_shape = [jax.ShapeDtypeStruct((b, s, n_main), F32)]
    out_specs = [pl.BlockSpec((1, tm, n_main), lambda i, j: (i, j, 0))]
    if n_all > n_main:
        out_shape.append(jax.ShapeDtypeStruct((b, s, n_all - n_main), F32))
        out_specs.append(pl.BlockSpec((1, tm, n_all - n_main), lambda i, j: (i, j, 0)))
    return pl.pallas_call(
        functools.partial(_inproj_kernel, n_main=n_main, tn=tn),
        out_shape=out_shape,
        grid=(b, s // tm),
        in_specs=[pl.BlockSpec((1, tm, d), lambda i, j: (i, j, 0)),
                  pl.BlockSpec((1, 1, d), lambda i, j: (i, 0, 0)),
                  pl.BlockSpec((1, 1, d), lambda i, j: (i, 0, 1)),
                  pl.BlockSpec((d, n_all), lambda i, j: (0, 0))],
        out_specs=out_specs,
        compiler_params=_params("parallel", "parallel"),
        name="inproj",
    )(x, mod, mod, w)


def _split_w_in(w, n_main):
    d, n = w.shape
    if n == n_main:
        return w.astype(BF16)
    pad = LANES - (n - n_main)
    return jnp.pad(w, ((0, 0), (0, pad))).astype(BF16)


def _gate_layouts(g, lo, width, chunk):
    b, s, _ = g.shape
    col = g[:, :, lo:lo + width].reshape(b, s // chunk, chunk, width)
    return col, jnp.swapaxes(col, 2, 3)


def _mlstm_kernel(q_ref, k_ref, v_ref, gcol_ref, grow_ref, bcol_ref, brow_ref, h_ref,
                  c_s, n_s, m_s, *, reverse, chunk, heads, dqk, dv):
    @pl.when(pl.program_id(1) == 0)
    def _():
        c_s[...] = jnp.zeros_like(c_s)
        n_s[...] = jnp.zeros_like(n_s)
        m_s[...] = jnp.zeros_like(m_s)

    ncb = gcol_ref.shape[1]
    row = lax.broadcasted_iota(jnp.int32, (chunk, chunk), 0)
    col = lax.broadcasted_iota(jnp.int32, (chunk, chunk), 1)
    mask = (col >= row) if reverse else (col <= row)
    mask_t = (row >= col) if reverse else (row <= col)
    scale = dqk ** -0.5

    def step(ci, carry):
        cc = (ncb - 1 - ci) if reverse else ci
        r0 = pl.multiple_of(cc * chunk, chunk)
        gcol = gcol_ref[0, cc] + bcol_ref[...]
        grow = grow_ref[0, cc] + brow_ref[...]
        fcol = -_softplus(-gcol)
        frow = -_softplus(-grow)
        for h in range(heads):
            i_col = gcol[:, h:h + 1]
            f_col = fcol[:, heads + h:heads + h + 1]
            i_row = grow[h:h + 1, :]
            f_row = frow[heads + h:heads + h + 1, :]
            b_col = jnp.sum(jnp.where(mask, f_row, 0.0), axis=1, keepdims=True)
            b_row = jnp.sum(jnp.where(mask_t, f_col, 0.0), axis=0, keepdims=True)
            total = jnp.sum(f_row, axis=1, keepdims=True)
            m_prev = m_s[h]
            dmat = jnp.where(mask, b_col + (i_row - b_row), -jnp.inf)
            inter = b_col + m_prev
            m_row = jnp.maximum(inter, jnp.max(dmat, axis=1, keepdims=True))
            w_intra = jnp.exp(dmat - m_row)
            w_inter = jnp.exp(inter - m_row)
            q = q_ref[0, pl.ds(r0, chunk), h * dqk:(h + 1) * dqk] * scale
            k = k_ref[0, pl.ds(r0, chunk), h * dqk:(h + 1) * dqk]
            v = v_ref[0, pl.ds(r0, chunk), h * dv:(h + 1) * dv]
            qb, kb, vb = q.astype(BF16), k.astype(BF16), v.astype(BF16)
            sqk = _mm_nt(qb, kb) * w_intra
            c_prev = c_s[h]
            n_prev = n_s[h]
            num = w_inter * _mm(qb, c_prev.astype(BF16)) + _mm(sqk.astype(BF16), vb)
            den = (w_inter * jnp.sum(q * n_prev, axis=1, keepdims=True)
                   + jnp.sum(sqk, axis=1, keepdims=True))
            h_ref[0, pl.ds(r0, chunk), h * dv:(h + 1) * dv] = (
                num / jnp.maximum(jnp.abs(den), jnp.exp(-m_row)))
            dlast = total + (i_col - b_col)
            m_new = jnp.maximum(total + m_prev, jnp.max(dlast, axis=0, keepdims=True))
            kw = k * jnp.exp(dlast - m_new)
            decay = jnp.exp(total + m_prev - m_new)
            c_s[h] = decay * c_prev + _mm_tn(kw.astype(BF16), vb)
            n_s[h] = decay * n_prev + jnp.sum(kw, axis=0, keepdims=True)
            m_s[h] = m_new
        return carry

    lax.fori_loop(0, ncb, step, 0)


def _mlstm_scan(proj, gcol, grow, bcol, brow, *, reverse):
    b, s, _ = proj.shape
    heads = A_HEADS
    dqk = proj.shape[2] // (6 * heads)
    dv = 2 * dqk
    nq, nv = heads * dqk, heads * dv
    blk = min(SCAN_BLOCK, s)
    nb = s // blk
    ncb = blk // SCAN_CHUNK
    ng = gcol.shape[3]

    def seq(j):
        return (nb - 1 - j) if reverse else j

    return pl.pallas_call(
        functools.partial(_mlstm_kernel, reverse=reverse, chunk=SCAN_CHUNK, heads=heads,
                          dqk=dqk, dv=dv),
        out_shape=jax.ShapeDtypeStruct((b, s, nv), F32),
        grid=(b, nb),
        in_specs=[pl.BlockSpec((1, blk, nq), lambda i, j: (i, seq(j), 0)),
                  pl.BlockSpec((1, blk, nq), lambda i, j: (i, seq(j), 1)),
                  pl.BlockSpec((1, blk, nv), lambda i, j: (i, seq(j), 1)),
                  pl.BlockSpec((1, ncb, SCAN_CHUNK, ng), lambda i, j: (i, seq(j), 0, 0)),
                  pl.BlockSpec((1, ncb, ng, SCAN_CHUNK), lambda i, j: (i, seq(j), 0, 0)),
                  pl.BlockSpec((1, ng), lambda i, j: (0, 0)),
                  pl.BlockSpec((ng, 1), lambda i, j: (0, 0))],
        out_specs=pl.BlockSpec((1, blk, nv), lambda i, j: (i, seq(j), 0)),
        scratch_shapes=[pltpu.VMEM((heads, dqk, dv), F32),
                        pltpu.VMEM((heads, 1, dqk), F32),
                        pltpu.VMEM((heads, 1, 1), F32)],
        compiler_params=_params("parallel", "arbitrary"),
        name="mlstm_scan_bwd" if reverse else "mlstm_scan_fwd",
    )(proj, proj, proj, gcol, grow, bcol, brow)


def _seq_halo_specs(tm, s, width, col_block=0):
    r = tm // HALO
    last = s // HALO - 1

    def prev(i, j):
        return (i, jnp.maximum(j * r - 1, 0), col_block)

    def nxt(i, j):
        return (i, jnp.minimum((j + 1) * r, last), col_block)

    return [pl.BlockSpec((1, HALO, width), prev),
            pl.BlockSpec((1, tm, width), lambda i, j: (i, j, col_block)),
            pl.BlockSpec((1, HALO, width), nxt)]


def _halo_valid(tm):
    j = pl.program_id(1)
    rid = lax.broadcasted_iota(jnp.int32, (tm + 2 * HALO, 1), 0)
    return jnp.logical_and(jnp.logical_or(rid >= HALO, j > 0),
                           jnp.logical_or(rid < tm + HALO, j < pl.num_programs(1) - 1))


def _dwconv_rows(ext, w_ref, c0, c1, taps, tm):
    n = ext.shape[0]
    acc = None
    for t in range(taps):
        off = t - taps // 2
        src = ext if off == 0 else pltpu.roll(ext, (-off) % n, 0)
        term = src * w_ref[t:t + 1, c0:c1]
        acc = term if acc is None else acc + term
    return acc[HALO:HALO + tm]


def _gdn_prep(proj, conv_w):
    b, s, n = proj.shape
    heads = B_HEADS
    dk = n // (4 * heads)
    width = 3 * heads * dk
    tm = min(ROW_TILE, s)
    specs = []
    for cb in range(3):
        specs += _seq_halo_specs(tm, s, heads * dk, cb)
    return pl.pallas_call(
        functools.partial(_gdn_prep_kernel, tm=tm, heads=heads, dk=dk),
        out_shape=jax.ShapeDtypeStruct((b, s, width), F32),
        grid=(b, s // tm),
        in_specs=specs + [pl.BlockSpec((B_CONV, width), lambda i, j: (0, 0))],
        out_specs=pl.BlockSpec((1, tm, width), lambda i, j: (i, j, 0)),
        compiler_params=_params("parallel", "parallel"),
        name="gdn_prep",
    )(*([proj] * 9), conv_w)


def _gdn_prep_kernel(qp, q, qn, kp, k, kn, vp, v, vn, w_ref, o_ref, *, tm, heads, dk):
    valid = _halo_valid(tm)
    hw = heads * dk
    for part, (p_ref, x_ref, n_ref) in enumerate(((qp, q, qn), (kp, k, kn), (vp, v, vn))):
        for h in range(heads):
            c0, c1 = h * dk, (h + 1) * dk
            ext = jnp.concatenate([p_ref[0, :, c0:c1], x_ref[0, :, c0:c1], n_ref[0, :, c0:c1]], axis=0)
            ext = jnp.where(valid, ext, 0.0)
            y = _dwconv_rows(ext, w_ref, part * hw + c0, part * hw + c1, B_CONV, tm)
            y = y * _sigmoid(y)
            if part < 2:
                y = y * lax.rsqrt(jnp.sum(y * y, axis=-1, keepdims=True) + EPS)
            if part == 0:
                y = y * (dk ** -0.5)
            o_ref[0, :, part * hw + c0:part * hw + c1] = y


def _unit_tri_inverse(a, eye, same_base, level_masks):
    hi = lax.Precision.HIGHEST

    def mm(x, y):
        return jnp.dot(x, y, preferred_element_type=F32, precision=hi)

    n1 = jnp.where(same_base, a, 0.0)
    n2 = mm(n1, n1)
    n4 = mm(n2, n2)
    x = mm(mm(eye - n1, eye + n2), eye + n4)
    for lm in level_masks:
        x = x - mm(mm(x, jnp.where(lm, a, 0.0)), x)
    return x


def _gdn_kernel(q_ref, k_ref, v_ref, gcol_ref, grow_ref, pcol_ref, prow_ref, o_ref, s_s,
                *, reverse, chunk, heads, dk):
    @pl.when(pl.program_id(1) == 0)
    def _():
        s_s[...] = jnp.zeros_like(s_s)

    ncb = gcol_ref.shape[1]
    row = lax.broadcasted_iota(jnp.int32, (chunk, chunk), 0)
    col = lax.broadcasted_iota(jnp.int32, (chunk, chunk), 1)
    mask = (col >= row) if reverse else (col <= row)
    mask_t = (row >= col) if reverse else (row <= col)
    strict = (col > row) if reverse else (col < row)
    eye = (row == col).astype(F32)
    base_log2 = 3

    def same_block(log2):
        return jnp.right_shift(row, log2) == jnp.right_shift(col, log2)

    same_base = same_block(base_log2)
    level_masks = []
    lg = base_log2
    while (1 << lg) < chunk:
        level_masks.append(jnp.logical_and(same_block(lg + 1), jnp.logical_not(same_block(lg))))
        lg += 1
    hi = lax.Precision.HIGHEST

    def step(ci, carry):
        cc = (ncb - 1 - ci) if reverse else ci
        r0 = pl.multiple_of(cc * chunk, chunk)
        raw_col = gcol_ref[0, cc]
        raw_row = grow_ref[0, cc]
        g_cols = -jnp.exp(pcol_ref[0:1, :]) * _softplus(raw_col[:, :heads] + pcol_ref[1:2, :])
        g_rows = -jnp.exp(prow_ref[:, 0:1]) * _softplus(raw_row[:heads, :] + prow_ref[:, 1:2])
        beta_cols = _sigmoid(raw_col[:, heads:])
        for h in range(heads):
            g_col = g_cols[:, h:h + 1]
            g_row = g_rows[h:h + 1, :]
            beta = beta_cols[:, h:h + 1]
            gc_col = jnp.sum(jnp.where(mask, g_row, 0.0), axis=1, keepdims=True)
            gc_row = jnp.sum(jnp.where(mask_t, g_col, 0.0), axis=0, keepdims=True)
            total = jnp.sum(g_row, axis=1, keepdims=True)
            decay = jnp.exp(jnp.where(mask, gc_col - gc_row, -jnp.inf))
            q = q_ref[0, pl.ds(r0, chunk), h * dk:(h + 1) * dk]
            k = k_ref[0, pl.ds(r0, chunk), h * dk:(h + 1) * dk]
            v = v_ref[0, pl.ds(r0, chunk), h * dk:(h + 1) * dk]
            kbeta = k * beta
            qb, kb = q.astype(BF16), k.astype(BF16)
            a = jnp.where(strict, _mm_nt(kbeta.astype(BF16), kb) * decay, 0.0)
            t_inv = _unit_tri_inverse(a, eye, same_base, level_masks)
            eg = jnp.exp(gc_col)
            rhs = jnp.concatenate([v * beta, kbeta * eg], axis=1)
            sol = jnp.dot(t_inv, rhs, preferred_element_type=F32, precision=hi)
            u, w = sol[:, :dk], sol[:, dk:]
            qk = _mm_nt(qb, kb) * decay
            qg = q * eg
            kd = k * jnp.exp(total - gc_col)
            s_prev = s_s[h]
            sb = s_prev.astype(BF16)
            v_new = u - _mm(w.astype(BF16), sb)
            vnb = v_new.astype(BF16)
            o_ref[0, pl.ds(r0, chunk), h * dk:(h + 1) * dk] = (
                _mm(qg.astype(BF16), sb) + _mm(qk.astype(BF16), vnb))
            s_s[h] = s_prev * jnp.exp(total) + _mm_tn(kd.astype(BF16), vnb)
        return carry

    lax.fori_loop(0, ncb, step, 0)


def _gdn_scan(qkv, gcol, grow, pcol, prow, *, reverse):
    b, s, n = qkv.shape
    heads = B_HEADS
    hw = n // 3
    dk = hw // heads
    blk = min(SCAN_BLOCK, s)
    nb = s // blk
    ncb = blk // SCAN_CHUNK
    ng = gcol.shape[3]

    def seq(j):
        return (nb - 1 - j) if reverse else j

    return pl.pallas_call(
        functools.partial(_gdn_kernel, reverse=reverse, chunk=SCAN_CHUNK, heads=heads, dk=dk),
        out_shape=jax.ShapeDtypeStruct((b, s, hw), F32),
        grid=(b, nb),
        in_specs=[pl.BlockSpec((1, blk, hw), lambda i, j: (i, seq(j), 0)),
                  pl.BlockSpec((1, blk, hw), lambda i, j: (i, seq(j), 1)),
                  pl.BlockSpec((1, blk, hw), lambda i, j: (i, seq(j), 2)),
                  pl.BlockSpec((1, ncb, SCAN_CHUNK, ng), lambda i, j: (i, seq(j), 0, 0)),
                  pl.BlockSpec((1, ncb, ng, SCAN_CHUNK), lambda i, j: (i, seq(j), 0, 0)),
                  pl.BlockSpec((2, heads), lambda i, j: (0, 0)),
                  pl.BlockSpec((heads, 2), lambda i, j: (0, 0))],
        out_specs=pl.BlockSpec((1, blk, hw), lambda i, j: (i, seq(j), 0)),
        scratch_shapes=[pltpu.VMEM((heads, dk, dk), F32)],
        compiler_params=_params("parallel", "arbitrary"),
        name="gdn_scan_bwd" if reverse else "gdn_scan_fwd",
    )(qkv, qkv, qkv, gcol, grow, pcol, prow)


def _rel_bucket(rel):
    nb = REL_BUCKETS // 2
    exact = nb // 2
    n = jnp.abs(rel)
    large = exact + (jnp.log(jnp.maximum(n, 1).astype(jnp.float32) / exact)
                     / math.log(REL_MAX_DIST / exact) * (nb - exact)).astype(jnp.int32)
    large = jnp.minimum(large, nb - 1)
    return jnp.where(rel > 0, nb, 0) + jnp.where(n < exact, n, large)


def _bias_tiles(rel_bias, t):
    assert t >= REL_MAX_DIST
    i = jnp.arange(t, dtype=jnp.int32)
    d = jnp.arange(-2, 3, dtype=jnp.int32)
    rel = d[:, None, None] * t + (i[None, None, :] - i[None, :, None])
    return jnp.transpose(rel_bias.astype(F32)[_rel_bucket(rel)], (3, 0, 1, 2))


def _attn_kernel(q_ref, k_ref, v_ref, bias_ref, lam_ref, ng_ref, o_ref,
                 m1_s, l1_s, a1_s, m2_s, l2_s, a2_s, *, scale, lambda_init, dh):
    ki = pl.program_id(3)

    @pl.when(ki == 0)
    def _():
        for m_s, l_s, a_s in ((m1_s, l1_s, a1_s), (m2_s, l2_s, a2_s)):
            m_s[...] = jnp.full_like(m_s, -jnp.inf)
            l_s[...] = jnp.zeros_like(l_s)
            a_s[...] = jnp.zeros_like(a_s)

    q = q_ref[0]
    k = k_ref[0]
    vb = v_ref[0].astype(BF16)
    bias = bias_ref[0, 0]
    for idx, (m_s, l_s, a_s) in enumerate(((m1_s, l1_s, a1_s), (m2_s, l2_s, a2_s))):
        qm = q[:, idx * dh:(idx + 1) * dh].astype(BF16)
        km = k[:, idx * dh:(idx + 1) * dh].astype(BF16)
        s = _mm_nt(qm, km) * scale + bias
        m_prev = m_s[...]
        m_new = jnp.maximum(m_prev, jnp.max(s, axis=1, keepdims=True))
        alpha = jnp.exp(m_prev - m_new)
        p = jnp.exp(s - m_new)
        l_s[...] = alpha * l_s[...] + jnp.sum(p, axis=1, keepdims=True)
        a_s[...] = alpha * a_s[...] + _mm(p.astype(BF16), vb)
        m_s[...] = m_new

    @pl.when(ki == pl.num_programs(3) - 1)
    def _():
        lp = lam_ref[...]
        lam = (jnp.exp(jnp.sum(lp[0:1] * lp[1:2], axis=1, keepdims=True))
               - jnp.exp(jnp.sum(lp[2:3] * lp[3:4], axis=1, keepdims=True)) + lambda_init)
        o = a1_s[...] / l1_s[...] - lam * (a2_s[...] / l2_s[...])
        o_ref[0] = _rms_rows(o) * ng_ref[...] * (1.0 - lambda_init)


def _diff_attn(proj, bias, lam_p, norm_g, lambda_init):
    b, s, n = proj.shape
    heads = C_HEADS
    dv = n // (3 * heads)
    dh = dv // 2
    t = bias.shape[2]
    nt = s // t

    def bias_idx(i, h, qi, ki):
        return (h, jnp.clip(ki - qi, -2, 2) + 2, 0, 0)

    return pl.pallas_call(
        functools.partial(_attn_kernel, scale=dh ** -0.5, lambda_init=lambda_init, dh=dh),
        out_shape=jax.ShapeDtypeStruct((b, s, heads * dv), F32),
        grid=(b, heads, nt, nt),
        in_specs=[pl.BlockSpec((1, t, dv), lambda i, h, qi, ki: (i, qi, h)),
                  pl.BlockSpec((1, t, dv), lambda i, h, qi, ki: (i, ki, heads + h)),
                  pl.BlockSpec((1, t, dv), lambda i, h, qi, ki: (i, ki, 2 * heads + h)),
                  pl.BlockSpec((1, 1, t, t), bias_idx),
                  pl.BlockSpec(lam_p.shape, lambda i, h, qi, ki: (0, 0)),
                  pl.BlockSpec((1, dv), lambda i, h, qi, ki: (0, 0))],
        out_specs=pl.BlockSpec((1, t, dv), lambda i, h, qi, ki: (i, qi, h)),
        scratch_shapes=[pltpu.VMEM((t, 1), F32), pltpu.VMEM((t, 1), F32), pltpu.VMEM((t, dv), F32),
                        pltpu.VMEM((t, 1), F32), pltpu.VMEM((t, 1), F32), pltpu.VMEM((t, dv), F32)],
        compiler_params=_params("parallel", "parallel", "parallel", "arbitrary"),
        name="diff_attn",
    )(proj, proj, proj, bias, lam_p, norm_g.reshape(1, dv))


def _outproj_kernel(*refs, mode, hd):
    if mode == "attn":
        a_ref, w_ref, x_ref, g_ref, o_ref = refs
        act = a_ref[0]
    else:
        a_ref, b_ref, gate_ref, ng_ref, w_ref, x_ref, g_ref, o_ref = refs
        ssum = a_ref[0] + b_ref[0]
        d = ssum.shape[1]
        hs = jnp.concatenate([_rms_rows(ssum[:, c0:c0 + hd]) for c0 in range(0, d, hd)], axis=1)
        hs = hs * ng_ref[...]
        gate = gate_ref[0]
        if mode == "mlstm":
            act = _sigmoid(gate) * hs
        else:
            act = hs * (gate * _sigmoid(gate))
    o_ref[0] = x_ref[0] + g_ref[0] * _mm(act.astype(BF16), w_ref[...])


def _outproj(mode, x, mod, w, a, b=None, gate_src=None, gate_block=0, norm_g=None, hd=0):
    bsz, s, d = x.shape
    tm = min(ROW_TILE, s)
    row = pl.BlockSpec((1, tm, d), lambda i, j: (i, j, 0))
    in_specs, args = [row], [a]
    if mode != "attn":
        in_specs += [row, pl.BlockSpec((1, tm, d), lambda i, j: (i, j, gate_block)),
                     pl.BlockSpec((1, d), lambda i, j: (0, 0))]
        args += [b, gate_src, norm_g.reshape(1, d)]
    in_specs += [pl.BlockSpec((d, d), lambda i, j: (0, 0)), row,
                 pl.BlockSpec((1, 1, d), lambda i, j: (i, 0, 2))]
    args += [w, x, mod]
    return pl.pallas_call(
        functools.partial(_outproj_kernel, mode=mode, hd=hd),
        out_shape=jax.ShapeDtypeStruct((bsz, s, d), F32),
        grid=(bsz, s // tm),
        in_specs=in_specs,
        out_specs=row,
        compiler_params=_params("parallel", "parallel"),
        name="outproj_" + mode,
    )(*args)


def _ffn_kernel(*refs, tm, dff, tn, final):
    if final:
        xp_ref, x_ref, xn_ref, sh_ref, sc_ref, g_ref, wu_ref, cw_ref, cb_ref, wd_ref, fg_ref, o_ref = refs
    else:
        xp_ref, x_ref, xn_ref, sh_ref, sc_ref, g_ref, wu_ref, cw_ref, cb_ref, wd_ref, o_ref = refs
    x = x_ref[0]
    ext = jnp.concatenate([xp_ref[0], x, xn_ref[0]], axis=0)
    hn = _rms_rows(ext) * (1.0 + sc_ref[0]) + sh_ref[0]
    hn = jnp.where(_halo_valid(tm), hn, 0.0).astype(BF16)
    acc = jnp.zeros_like(x)
    for c0 in range(0, dff, tn):
        ua = _dwconv_rows(_mm(hn, wu_ref[:, c0:c0 + tn]), cw_ref, c0, c0 + tn, FFN_CONV, tm)
        ug = _dwconv_rows(_mm(hn, wu_ref[:, dff + c0:dff + c0 + tn]), cw_ref, dff + c0, dff + c0 + tn,
                          FFN_CONV, tm)
        ua = ua + cb_ref[:, c0:c0 + tn]
        ug = ug + cb_ref[:, dff + c0:dff + c0 + tn]
        act = ua * (ug * _sigmoid(ug))
        acc = acc + _mm(act.astype(BF16), wd_ref[c0:c0 + tn, :])
    y = x + g_ref[0] * acc
    if final:
        y = _rms_rows(y) * fg_ref[...]
    o_ref[0] = y


def _conv_ffn(x, mod, w_up, conv_w, conv_b, w_down, final_g=None):
    b, s, d = x.shape
    dff = w_down.shape[0]
    tm = min(ROW_TILE, s)
    final = final_g is not None
    const = lambda i, j: (0, 0)
    in_specs = _seq_halo_specs(tm, s, d) + [
        pl.BlockSpec((1, 1, d), lambda i, j: (i, 0, 3)),
        pl.BlockSpec((1, 1, d), lambda i, j: (i, 0, 4)),
        pl.BlockSpec((1, 1, d), lambda i, j: (i, 0, 5)),
        pl.BlockSpec(w_up.shape, const),
        pl.BlockSpec(conv_w.shape, const),
        pl.BlockSpec((1, 2 * dff), const),
        pl.BlockSpec(w_down.shape, const)]
    args = [x, x, x, mod, mod, mod, w_up, conv_w, conv_b.reshape(1, 2 * dff), w_down]
    if final:
        in_specs.append(pl.BlockSpec((1, d), const))
        args.append(final_g.reshape(1, d))
    return pl.pallas_call(
        functools.partial(_ffn_kernel, tm=tm, dff=dff, tn=FFN_COLS, final=final),
        out_shape=jax.ShapeDtypeStruct((b, s, d), F32),
        grid=(b, s // tm),
        in_specs=in_specs,
        out_specs=pl.BlockSpec((1, tm, d), lambda i, j: (i, j, 0)),
        compiler_params=_params("parallel", "parallel"),
        name="conv_ffn",
    )(*args)


def _mlstm_layer(x, mod, w_in, gate_b, norm_g, w_out):
    heads = A_HEADS
    n_main = w_in.shape[1] - 4 * heads
    proj, gates = _inproj(x, mod, _split_w_in(w_in, n_main), n_main)
    hs = []
    for direction in range(2):
        lo = 2 * heads * direction
        gcol, grow = _gate_layouts(gates, lo, 2 * heads, SCAN_CHUNK)
        bias = gate_b[lo:lo + 2 * heads].astype(F32)
        hs.append(_mlstm_scan(proj, gcol, grow, bias.reshape(1, -1), bias.reshape(-1, 1),
                              reverse=direction == 1))
    d = x.shape[2]
    return _outproj("mlstm", x, mod, w_out.astype(BF16), hs[0], hs[1], gate_src=proj,
                    gate_block=n_main // d - 1, norm_g=norm_g, hd=d // heads)


def _gdn_layer(x, mod, w_in, conv_w, a_log, dt_bias, norm_g, w_out):
    heads = B_HEADS
    n_main = w_in.shape[1] - 4 * heads
    proj, gates = _inproj(x, mod, _split_w_in(w_in, n_main), n_main)
    qkv = _gdn_prep(proj, conv_w)
    outs = []
    for direction in range(2):
        gcol, grow = _gate_layouts(gates, 2 * heads * direction, 2 * heads, SCAN_CHUNK)
        p = jnp.stack([a_log[direction], dt_bias[direction]]).astype(F32)
        outs.append(_gdn_scan(qkv, gcol, grow, p, p.T, reverse=direction == 1))
    d = x.shape[2]
    return _outproj("gdn", x, mod, w_out.astype(BF16), outs[0], outs[1], gate_src=proj,
                    gate_block=n_main // d - 1, norm_g=jnp.tile(norm_g, heads), hd=d // heads)


def _attn_layer(x, mod, w_in, lam_p, norm_g, w_out, bias, lambda_init):
    proj = _inproj(x, mod, w_in.astype(BF16), w_in.shape[1])[0]
    o = _diff_attn(proj, bias, lam_p.astype(F32), norm_g, lambda_init)
    return _outproj("attn", x, mod, w_out.astype(BF16), o)


def _trunk(x, mods, bias_tiles, ada_w, ada_b, a_w_in, a_gate_b, a_norm_g, a_w_out, b_w_in, b_conv_w,
           b_a_log, b_dt_bias, b_norm_g, b_w_out, c_w_in, c_lambda, c_norm_g, c_w_out, rel_bias,
           ffn_w_up, ffn_conv_w, ffn_conv_b, ffn_w_down, final_g):
    depth = ada_w.shape[0]
    for i in range(depth):
        mod = mods[i][:, None, :]
        j = i // N_MIXERS
        kind = i % N_MIXERS
        if kind == 0:
            x = _mlstm_layer(x, mod, a_w_in[j], a_gate_b[j], a_norm_g[j], a_w_out[j])
        elif kind == 1:
            x = _gdn_layer(x, mod, b_w_in[j], b_conv_w[j], b_a_log[j], b_dt_bias[j], b_norm_g[j],
                           b_w_out[j])
        else:
            x = _attn_layer(x, mod, c_w_in[j], c_lambda[j], c_norm_g[j], c_w_out[j], bias_tiles,
                            0.8 - 0.6 * math.exp(-0.3 * i))
        x = _conv_ffn(x, mod, ffn_w_up[i].astype(BF16), ffn_conv_w[i], ffn_conv_b[i],
                      ffn_w_down[i].astype(BF16), final_g if i == depth - 1 else None)
    return x


def kernel(x_prompt, x_sample, c_prompt, c_sample, ada_w, ada_b, a_w_in, a_gate_b, a_norm_g, a_w_out, b_w_in, b_conv_w, b_a_log, b_dt_bias, b_norm_g, b_w_out, c_w_in, c_lambda, c_norm_g, c_w_out, rel_bias, ffn_w_up, ffn_conv_w, ffn_conv_b, ffn_w_down, final_g):
    weights = (ada_w, ada_b, a_w_in, a_gate_b, a_norm_g, a_w_out, b_w_in, b_conv_w, b_a_log,
               b_dt_bias, b_norm_g, b_w_out, c_w_in, c_lambda, c_norm_g, c_w_out, rel_bias,
               ffn_w_up, ffn_conv_w, ffn_conv_b, ffn_w_down, final_g)
    nb_p, nb_s = c_prompt.shape[0], c_sample.shape[0]
    c_all = jnp.concatenate([c_prompt, c_sample], axis=0)
    pad = (-c_all.shape[0]) % SUBLANES
    mods = _ada_mod(jnp.pad(c_all, ((0, pad), (0, 0))), ada_w, ada_b)
    outs, bias_tiles = [], {}
    for x, lo, n in ((x_prompt, 0, nb_p), (x_sample, nb_p, nb_s)):
        t = min(ATTN_TILE, x.shape[1])
        if t not in bias_tiles:
            bias_tiles[t] = _bias_tiles(rel_bias, t)
        outs.append(_trunk(x, mods[:, lo:lo + n], bias_tiles[t], *weights))
    return tuple(outs)
```

```python
import functools
import math

import jax
import jax.numpy as jnp
from jax import lax
from jax.experimental import pallas as pl
from jax.experimental.pallas import tpu as pltpu

F32 = jnp.float32
BF16 = jnp.bfloat16
EPS = 1e-6

N_MIXERS = 3
A_HEADS = 4
B_HEADS = 8
B_CONV = 5
C_HEADS = 8
REL_BUCKETS = 32
REL_MAX_DIST = 128
FFN_CONV = 3

LANES = 128
SUBLANES = 8
HALO = SUBLANES
VMEM_LIMIT = 56 * 1024 * 1024

ROW_TILE = 512
SCAN_BLOCK = 512
SCAN_CHUNK = 64
GDN_GROUP = 256
MLSTM_CHUNK = 256
ATTN_TILE = 512
ATTN_KEYS = 2048
FFN_COLS = 1408


def _params(*sem):
    return pltpu.CompilerParams(dimension_semantics=sem, vmem_limit_bytes=VMEM_LIMIT)


def _sigmoid(x):
    return 1.0 / (1.0 + jnp.exp(-x))


def _softplus(x):
    return jnp.maximum(x, 0.0) + jnp.log(1.0 + jnp.exp(-jnp.abs(x)))


def _rms_rows(x):
    return x * lax.rsqrt(jnp.mean(x * x, axis=-1, keepdims=True) + EPS)


def _mm(a, b):
    return jnp.dot(a, b, preferred_element_type=F32)


def _mm_inv(a, b):
    return _mm(a.astype(BF16), b.astype(BF16))


def _mm_nt(a, b):
    return lax.dot_general(a, b, (((1,), (1,)), ((), ())), preferred_element_type=F32)


def _mm_tn(a, b):
    return lax.dot_general(a, b, (((0,), (0,)), ((), ())), preferred_element_type=F32)


def _mod_kernel(c_ref, w_ref, b_ref, o_ref):
    c = c_ref[...]
    cs = c * _sigmoid(c)
    o_ref[0] = _mm(cs.astype(BF16), w_ref[0].astype(BF16)) + b_ref[0]


def _ada_mod(c, ada_w, ada_b):
    depth, d, n = ada_w.shape
    bp = c.shape[0]
    tn = 1536
    return pl.pallas_call(
        _mod_kernel,
        out_shape=jax.ShapeDtypeStruct((depth, bp, n), F32),
        grid=(depth, n // tn),
        in_specs=[pl.BlockSpec((bp, d), lambda l, j: (0, 0)),
                  pl.BlockSpec((1, d, tn), lambda l, j: (l, 0, j)),
                  pl.BlockSpec((1, 1, tn), lambda l, j: (l, 0, j))],
        out_specs=pl.BlockSpec((1, bp, tn), lambda l, j: (l, 0, j)),
        compiler_params=_params("parallel", "parallel"),
        name="ada_mod",
    )(c, ada_w, ada_b.reshape(depth, 1, n))


def _inproj_kernel(x_ref, sh_ref, sc_ref, w_ref, *o_refs, n_main, tn):
    hn = (_rms_rows(x_ref[0]) * (1.0 + sc_ref[0]) + sh_ref[0]).astype(BF16)
    for c0 in range(0, n_main, tn):
        o_refs[0][0, :, c0:c0 + tn] = _mm(hn, w_ref[:, c0:c0 + tn])
    if len(o_refs) > 1:
        o_refs[1][0] = _mm(hn, w_ref[:, n_main:])


def _inproj(x, mod, w, n_main):
    b, s, d = x.shape
    n_all = w.shape[1]
    tm = min(ROW_TILE, s)
    tn = 512
    out_shape = [jax.ShapeDtypeStruct((b, s, n_main), F32)]
    out_specs = [pl.BlockSpec((1, tm, n_main), lambda i, j: (i, j, 0))]
    if n_all > n_main:
        out_shape.append(jax.ShapeDtypeStruct((b, s, n_all - n_main), F32))
        out_specs.append(pl.BlockSpec((1, tm, n_all - n_main), lambda i, j: (i, j, 0)))
    return pl.pallas_call(
        functools.partial(_inproj_kernel, n_main=n_main, tn=tn),
        out_shape=out_shape,
        grid=(b, s // tm),
        in_specs=[pl.BlockSpec((1, tm, d), lambda i, j: (i, j, 0)),
                  pl.BlockSpec((1, 1, d), lambda i, j: (i, 0, 0)),
                  pl.BlockSpec((1, 1, d), lambda i, j: (i, 0, 1)),
                  pl.BlockSpec((d, n_all), lambda i, j: (0, 0))],
        out_specs=out_specs,
        compiler_params=_params("parallel", "parallel"),
        name="inproj",
    )(x, mod, mod, w)


def _split_w_in(w, n_main):
    d, n = w.shape
    if n == n_main:
        return w.astype(BF16)
    pad = LANES - (n - n_main)
    return jnp.pad(w, ((0, 0), (0, pad))).astype(BF16)


def _gate_layouts(g, lo, width, chunk):
    b, s, _ = g.shape
    col = g[:, :, lo:lo + width].reshape(b, s // chunk, chunk, width)
    return col, jnp.swapaxes(col, 2, 3)


def _mlstm_kernel(q_ref, k_ref, v_ref, gcol_ref, grow_ref, bcol_ref, brow_ref, h_ref,
                  c_s, n_s, m_s, *, reverse, chunk, heads, dqk, dv):
    @pl.when(pl.program_id(1) == 0)
    def _():
        c_s[...] = jnp.zeros_like(c_s)
        n_s[...] = jnp.zeros_like(n_s)
        m_s[...] = jnp.zeros_like(m_s)

    ncb = gcol_ref.shape[1]
    row = lax.broadcasted_iota(jnp.int32, (chunk, chunk), 0)
    col = lax.broadcasted_iota(jnp.int32, (chunk, chunk), 1)
    mask = (col >= row) if reverse else (col <= row)
    mask_t = (row >= col) if reverse else (row <= col)
    scale = dqk ** -0.5

    def step(ci, carry):
        cc = (ncb - 1 - ci) if reverse else ci
        r0 = pl.multiple_of(cc * chunk, chunk)
        gcol = gcol_ref[0, cc] + bcol_ref[...]
        grow = grow_ref[0, cc] + brow_ref[...]
        fcol = -_softplus(-gcol)
        frow = -_softplus(-grow)
        for h in range(heads):
            i_col = gcol[:, h:h + 1]
            f_col = fcol[:, heads + h:heads + h + 1]
            i_row = grow[h:h + 1, :]
            f_row = frow[heads + h:heads + h + 1, :]
            b_col = jnp.sum(jnp.where(mask, f_row, 0.0), axis=1, keepdims=True)
            b_row = jnp.sum(jnp.where(mask_t, f_col, 0.0), axis=0, keepdims=True)
            total = jnp.sum(f_row, axis=1, keepdims=True)
            m_prev = m_s[h]
            dmat = jnp.where(mask, b_col + (i_row - b_row), -jnp.inf)
            inter = b_col + m_prev
            m_row = jnp.maximum(inter, jnp.max(dmat, axis=1, keepdims=True))
            w_intra = jnp.exp(dmat - m_row)
            w_inter = jnp.exp(inter - m_row)
            q = q_ref[0, pl.ds(r0, chunk), h * dqk:(h + 1) * dqk] * scale
            k = k_ref[0, pl.ds(r0, chunk), h * dqk:(h + 1) * dqk]
            v = v_ref[0, pl.ds(r0, chunk), h * dv:(h + 1) * dv]
            qb, kb, vb = q.astype(BF16), k.astype(BF16), v.astype(BF16)
            sqk = _mm_nt(qb, kb) * w_intra
            c_prev = c_s[h]
            n_prev = n_s[h]
            num = w_inter * _mm(qb, c_prev.astype(BF16)) + _mm(sqk.astype(BF16), vb)
            den = (w_inter * jnp.sum(q * n_prev, axis=1, keepdims=True)
                   + jnp.sum(sqk, axis=1, keepdims=True))
            h_ref[0, pl.ds(r0, chunk), h * dv:(h + 1) * dv] = (
                num / jnp.maximum(jnp.abs(den), jnp.exp(-m_row)))
            dlast = total + (i_col - b_col)
            m_new = jnp.maximum(total + m_prev, jnp.max(dlast, axis=0, keepdims=True))
            kw = k * jnp.exp(dlast - m_new)
            decay = jnp.exp(total + m_prev - m_new)
            c_s[h] = decay * c_prev + _mm_tn(kw.astype(BF16), vb)
            n_s[h] = decay * n_prev + jnp.sum(kw, axis=0, keepdims=True)
            m_s[h] = m_new
        return carry

    lax.fori_loop(0, ncb, step, 0)


def _mlstm_scan(proj, gcol, grow, bcol, brow, *, reverse):
    b, s, _ = proj.shape
    heads = A_HEADS
    dqk = proj.shape[2] // (6 * heads)
    dv = 2 * dqk
    nq, nv = heads * dqk, heads * dv
    blk = min(SCAN_BLOCK, s)
    nb = s // blk
    chunk = gcol.shape[2]
    ncb = blk // chunk
    ng = gcol.shape[3]

    def seq(j):
        return (nb - 1 - j) if reverse else j

    return pl.pallas_call(
        functools.partial(_mlstm_kernel, reverse=reverse, chunk=chunk, heads=heads,
                          dqk=dqk, dv=dv),
        out_shape=jax.ShapeDtypeStruct((b, s, nv), F32),
        grid=(b, nb),
        in_specs=[pl.BlockSpec((1, blk, nq), lambda i, j: (i, seq(j), 0)),
                  pl.BlockSpec((1, blk, nq), lambda i, j: (i, seq(j), 1)),
                  pl.BlockSpec((1, blk, nv), lambda i, j: (i, seq(j), 1)),
                  pl.BlockSpec((1, ncb, chunk, ng), lambda i, j: (i, seq(j), 0, 0)),
                  pl.BlockSpec((1, ncb, ng, chunk), lambda i, j: (i, seq(j), 0, 0)),
                  pl.BlockSpec((1, ng), lambda i, j: (0, 0)),
                  pl.BlockSpec((ng, 1), lambda i, j: (0, 0))],
        out_specs=pl.BlockSpec((1, blk, nv), lambda i, j: (i, seq(j), 0)),
        scratch_shapes=[pltpu.VMEM((heads, dqk, dv), F32),
                        pltpu.VMEM((heads, 1, dqk), F32),
                        pltpu.VMEM((heads, 1, 1), F32)],
        compiler_params=_params("parallel", "arbitrary"),
        name="mlstm_scan_bwd" if reverse else "mlstm_scan_fwd",
    )(proj, proj, proj, gcol, grow, bcol, brow)


def _seq_halo_specs(tm, s, width, col_block=0):
    r = tm // HALO
    last = s // HALO - 1

    def prev(i, j):
        return (i, jnp.maximum(j * r - 1, 0), col_block)

    def nxt(i, j):
        return (i, jnp.minimum((j + 1) * r, last), col_block)

    return [pl.BlockSpec((1, HALO, width), prev),
            pl.BlockSpec((1, tm, width), lambda i, j: (i, j, col_block)),
            pl.BlockSpec((1, HALO, width), nxt)]


def _halo_valid(tm):
    j = pl.program_id(1)
    rid = lax.broadcasted_iota(jnp.int32, (tm + 2 * HALO, 1), 0)
    return jnp.logical_and(jnp.logical_or(rid >= HALO, j > 0),
                           jnp.logical_or(rid < tm + HALO, j < pl.num_programs(1) - 1))


def _dwconv_rows(ext, w_ref, c0, c1, taps, tm):
    n = ext.shape[0]
    acc = None
    for t in range(taps):
        off = t - taps // 2
        src = ext if off == 0 else pltpu.roll(ext, (-off) % n, 0)
        term = src * w_ref[t:t + 1, c0:c1]
        acc = term if acc is None else acc + term
    return acc[HALO:HALO + tm]


def _gdn_prep(proj, conv_w):
    b, s, n = proj.shape
    heads = B_HEADS
    dk = n // (4 * heads)
    width = 3 * heads * dk
    tm = min(ROW_TILE, s)
    specs = []
    for cb in range(3):
        specs += _seq_halo_specs(tm, s, heads * dk, cb)
    return pl.pallas_call(
        functools.partial(_gdn_prep_kernel, tm=tm, heads=heads, dk=dk),
        out_shape=jax.ShapeDtypeStruct((b, s, width), F32),
        grid=(b, s // tm),
        in_specs=specs + [pl.BlockSpec((B_CONV, width), lambda i, j: (0, 0))],
        out_specs=pl.BlockSpec((1, tm, width), lambda i, j: (i, j, 0)),
        compiler_params=_params("parallel", "parallel"),
        name="gdn_prep",
    )(*([proj] * 9), conv_w)


def _gdn_prep_kernel(qp, q, qn, kp, k, kn, vp, v, vn, w_ref, o_ref, *, tm, heads, dk):
    valid = _halo_valid(tm)
    hw = heads * dk
    for part, (p_ref, x_ref, n_ref) in enumerate(((qp, q, qn), (kp, k, kn), (vp, v, vn))):
        for h in range(heads):
            c0, c1 = h * dk, (h + 1) * dk
            ext = jnp.concatenate([p_ref[0, :, c0:c1], x_ref[0, :, c0:c1], n_ref[0, :, c0:c1]], axis=0)
            ext = jnp.where(valid, ext, 0.0)
            y = _dwconv_rows(ext, w_ref, part * hw + c0, part * hw + c1, B_CONV, tm)
            y = y * _sigmoid(y)
            if part < 2:
                y = y * lax.rsqrt(jnp.sum(y * y, axis=-1, keepdims=True) + EPS)
            if part == 0:
                y = y * (dk ** -0.5)
            o_ref[0, :, part * hw + c0:part * hw + c1] = y


def _unit_tri_inverse(a, eye, same_base, level_masks):
    mm = _mm_inv
    n1 = jnp.where(same_base, a, 0.0)
    n2 = mm(n1, n1)
    n4 = mm(n2, n2)
    x = mm(mm(eye - n1, eye + n2), eye + n4)
    for lm in level_masks:
        x = x - mm(mm(x, jnp.where(lm, a, 0.0)), x)
    return x


def _gdn_kernel(q_ref, k_ref, v_ref, gcol_ref, grow_ref, pcol_ref, prow_ref, o_ref,
                s_s, mq_s, r_s, gl_s, *, reverse, chunk, group, heads, dk):
    @pl.when(pl.program_id(1) == 0)
    def _():
        s_s[...] = jnp.zeros_like(s_s)

    blk = q_ref.shape[1]
    cpg = group // chunk
    clog = chunk.bit_length() - 1
    row = lax.broadcasted_iota(jnp.int32, (group, group), 0)
    col = lax.broadcasted_iota(jnp.int32, (group, group), 1)

    def same_block(log2):
        return jnp.right_shift(row, log2) == jnp.right_shift(col, log2)

    same_chunk = same_block(clog)
    causal = jnp.logical_and(same_chunk, (col >= row) if reverse else (col <= row))
    causal_t = jnp.logical_and(same_chunk, (row >= col) if reverse else (row <= col))
    strict = jnp.logical_and(same_chunk, (col > row) if reverse else (col < row))
    eye = (row == col).astype(F32)
    base_log2 = 3
    same_base = same_block(base_log2)
    level_masks = [jnp.logical_and(same_block(lg + 1), jnp.logical_not(same_block(lg)))
                   for lg in range(base_log2, clog)]
    chunk_of_row = jnp.right_shift(lax.broadcasted_iota(jnp.int32, (group, 1), 0), clog)

    raw_col = gcol_ref[0]
    raw_row = grow_ref[0]
    g_cols = -jnp.exp(pcol_ref[0:1, :]) * _softplus(raw_col[:, :heads] + pcol_ref[1:2, :])
    g_rows = -jnp.exp(prow_ref[:, 0:1]) * _softplus(raw_row[:heads, :] + prow_ref[:, 1:2])
    beta_cols = _sigmoid(raw_col[:, heads:])

    for h in range(heads):
        for gi in range(blk // group):
            r0 = gi * group
            g_col = g_cols[r0:r0 + group, h:h + 1]
            g_row = g_rows[h:h + 1, r0:r0 + group]
            beta = beta_cols[r0:r0 + group, h:h + 1]
            gc_col = jnp.sum(jnp.where(causal, g_row, 0.0), axis=1, keepdims=True)
            gc_row = jnp.sum(jnp.where(causal_t, g_col, 0.0), axis=0, keepdims=True)
            tot_col = jnp.sum(jnp.where(same_chunk, g_row, 0.0), axis=1, keepdims=True)
            decay = jnp.exp(jnp.where(causal, gc_col - gc_row, -jnp.inf))
            q = q_ref[0, r0:r0 + group, h * dk:(h + 1) * dk]
            k = k_ref[0, r0:r0 + group, h * dk:(h + 1) * dk]
            v = v_ref[0, r0:r0 + group, h * dk:(h + 1) * dk]
            kb = k.astype(BF16)
            kq = _mm_nt(jnp.concatenate([kb, q.astype(BF16)], axis=0), kb)
            a = jnp.where(strict, kq[:group] * decay, 0.0) * beta
            t_inv = _unit_tri_inverse(a, eye, same_base, level_masks)
            eg = jnp.exp(gc_col)
            kbeta = k * beta
            wu = _mm_inv(t_inv, jnp.concatenate([kbeta * eg, v * beta], axis=1)).astype(BF16)
            qo = _mm((kq[group:] * decay).astype(BF16), wu)
            o_ref[0, r0:r0 + group, h * dk:(h + 1) * dk] = qo[:, dk:]
            q_prime = (q * eg - qo[:, :dk]).astype(BF16)
            kd = k * jnp.exp(tot_col - gc_col)
            kd_by_chunk = jnp.concatenate(
                [jnp.where(chunk_of_row == c, kd, 0.0) for c in range(cpg)], axis=1).astype(BF16)
            mr = _mm_tn(kd_by_chunk, wu)
            for c in range(cpg):
                cg = gi * cpg + c
                mq_s[h, cg, 0:dk, :] = mr[c * dk:(c + 1) * dk, :dk].astype(BF16)
                mq_s[h, cg, dk:dk + chunk, :] = q_prime[c * chunk:(c + 1) * chunk]
                r_s[h, cg] = mr[c * dk:(c + 1) * dk, dk:]
                total = jnp.sum(g_row[:, c * chunk:(c + 1) * chunk], axis=1, keepdims=True)
                gl_s[h, cg] = jnp.broadcast_to(jnp.exp(total), (1, dk))

    ncb = blk // chunk
    for ci in range(ncb):
        cc = (ncb - 1 - ci) if reverse else ci
        for h in range(heads):
            s_prev = s_s[h]
            p = _mm(mq_s[h, cc], s_prev.astype(BF16))
            s_s[h] = gl_s[h, cc] * s_prev - p[:dk] + r_s[h, cc]
            o_ref[0, cc * chunk:(cc + 1) * chunk, h * dk:(h + 1) * dk] += p[dk:]


def _gdn_scan(qkv, gcol, grow, pcol, prow, *, reverse):
    b, s, n = qkv.shape
    heads = B_HEADS
    hw = n // 3
    dk = hw // heads
    blk = min(SCAN_BLOCK, s)
    nb = s // blk
    ncb = blk // SCAN_CHUNK
    ng = gcol.shape[2]

    def seq(j):
        return (nb - 1 - j) if reverse else j

    return pl.pallas_call(
        functools.partial(_gdn_kernel, reverse=reverse, chunk=SCAN_CHUNK, group=GDN_GROUP,
                          heads=heads, dk=dk),
        out_shape=jax.ShapeDtypeStruct((b, s, hw), F32),
        grid=(b, nb),
        in_specs=[pl.BlockSpec((1, blk, hw), lambda i, j: (i, seq(j), 0)),
                  pl.BlockSpec((1, blk, hw), lambda i, j: (i, seq(j), 1)),
                  pl.BlockSpec((1, blk, hw), lambda i, j: (i, seq(j), 2)),
                  pl.BlockSpec((1, blk, ng), lambda i, j: (i, seq(j), 0)),
                  pl.BlockSpec((1, ng, blk), lambda i, j: (i, 0, seq(j))),
                  pl.BlockSpec((2, heads), lambda i, j: (0, 0)),
                  pl.BlockSpec((heads, 2), lambda i, j: (0, 0))],
        out_specs=pl.BlockSpec((1, blk, hw), lambda i, j: (i, seq(j), 0)),
        scratch_shapes=[pltpu.VMEM((heads, dk, dk), F32),
                        pltpu.VMEM((heads, ncb, dk + SCAN_CHUNK, dk), BF16),
                        pltpu.VMEM((heads, ncb, dk, dk), F32),
                        pltpu.VMEM((heads, ncb, 1, dk), F32)],
        compiler_params=_params("parallel", "arbitrary"),
        name="gdn_scan_bwd" if reverse else "gdn_scan_fwd",
    )(qkv, qkv, qkv, gcol, grow, pcol, prow)


def _rel_bucket(rel):
    nb = REL_BUCKETS // 2
    exact = nb // 2
    n = jnp.abs(rel)
    large = exact + (jnp.log(jnp.maximum(n, 1).astype(jnp.float32) / exact)
                     / math.log(REL_MAX_DIST / exact) * (nb - exact)).astype(jnp.int32)
    large = jnp.minimum(large, nb - 1)
    return jnp.where(rel > 0, nb, 0) + jnp.where(n < exact, n, large)


def _bias_tiles(rel_bias, t):
    assert t >= REL_MAX_DIST
    n = 2 * t + 1
    m = jnp.arange(n, dtype=jnp.int32)
    rel_in_tile = jnp.where(m < t, m, m - n)
    d = jnp.arange(-2, 3, dtype=jnp.int32)
    rel = d[:, None] * t + rel_in_tile[None, :]
    vals = jnp.transpose(rel_bias.astype(F32)[_rel_bucket(rel)], (2, 0, 1))
    h = vals.shape[0]
    tiled = jnp.tile(vals, (1, 1, t))[:, :, :t * (n - 1)].reshape(h, 5, t, n - 1)
    return tiled[:, :, :, :t]


def _attn_kernel(q_ref, k_ref, v_ref, bias_ref, lam_ref, ng_ref, o_ref,
                 m1_s, l1_s, a1_s, m2_s, l2_s, a2_s, *, scale, lambda_init, dh):
    ki = pl.program_id(3)

    @pl.when(ki == 0)
    def _():
        for m_s, l_s, a_s in ((m1_s, l1_s, a1_s), (m2_s, l2_s, a2_s)):
            m_s[...] = jnp.full_like(m_s, -jnp.inf)
            l_s[...] = jnp.zeros_like(l_s)
            a_s[...] = jnp.zeros_like(a_s)

    q = q_ref[0] * scale
    k = k_ref[0]
    vb = v_ref[0].astype(BF16)
    t = q.shape[0]
    r = k.shape[0] // t
    first = ki * r - pl.program_id(2)
    bias = jnp.concatenate([bias_ref[0, jnp.clip(first + j, -2, 2) + 2] for j in range(r)], axis=1)
    for idx, (m_s, l_s, a_s) in enumerate(((m1_s, l1_s, a1_s), (m2_s, l2_s, a2_s))):
        qm = q[:, idx * dh:(idx + 1) * dh].astype(BF16)
        km = k[:, idx * dh:(idx + 1) * dh].astype(BF16)
        s = _mm_nt(qm, km) + bias
        m_prev = m_s[...]
        m_new = jnp.maximum(m_prev, jnp.max(s, axis=1, keepdims=True))
        alpha = jnp.exp(m_prev - m_new)
        p = jnp.exp(s - m_new)
        l_s[...] = alpha * l_s[...] + jnp.sum(p, axis=1, keepdims=True)
        a_s[...] = alpha * a_s[...] + _mm(p.astype(BF16), vb)
        m_s[...] = m_new

    @pl.when(ki == pl.num_programs(3) - 1)
    def _():
        lp = lam_ref[...]
        lam = (jnp.exp(jnp.sum(lp[0:1] * lp[1:2], axis=1, keepdims=True))
               - jnp.exp(jnp.sum(lp[2:3] * lp[3:4], axis=1, keepdims=True)) + lambda_init)
        o = a1_s[...] / l1_s[...] - lam * (a2_s[...] / l2_s[...])
        o_ref[0] = _rms_rows(o) * ng_ref[...] * (1.0 - lambda_init)


def _diff_attn(proj, bias, lam_p, norm_g, lambda_init):
    b, s, n = proj.shape
    heads = C_HEADS
    dv = n // (3 * heads)
    dh = dv // 2
    t = bias.shape[2]
    tk = min(ATTN_KEYS, s)
    scale = dh ** -0.5
    assert math.frexp(scale)[0] == 0.5, "q pre-scaling assumes a power-of-two softmax scale"

    return pl.pallas_call(
        functools.partial(_attn_kernel, scale=scale, lambda_init=lambda_init, dh=dh),
        out_shape=jax.ShapeDtypeStruct((b, s, heads * dv), F32),
        grid=(b, heads, s // t, s // tk),
        in_specs=[pl.BlockSpec((1, t, dv), lambda i, h, qi, ki: (i, qi, h)),
                  pl.BlockSpec((1, tk, dv), lambda i, h, qi, ki: (i, ki, heads + h)),
                  pl.BlockSpec((1, tk, dv), lambda i, h, qi, ki: (i, ki, 2 * heads + h)),
                  pl.BlockSpec((1, 5, t, t), lambda i, h, qi, ki: (h, 0, 0, 0)),
                  pl.BlockSpec(lam_p.shape, lambda i, h, qi, ki: (0, 0)),
                  pl.BlockSpec((1, dv), lambda i, h, qi, ki: (0, 0))],
        out_specs=pl.BlockSpec((1, t, dv), lambda i, h, qi, ki: (i, qi, h)),
        scratch_shapes=[pltpu.VMEM((t, 1), F32), pltpu.VMEM((t, 1), F32), pltpu.VMEM((t, dv), F32),
                        pltpu.VMEM((t, 1), F32), pltpu.VMEM((t, 1), F32), pltpu.VMEM((t, dv), F32)],
        compiler_params=_params("parallel", "parallel", "parallel", "arbitrary"),
        name="diff_attn",
    )(proj, proj, proj, bias, lam_p, norm_g.reshape(1, dv))


def _outproj_kernel(*refs, mode, hd):
    if mode == "attn":
        a_ref, w_ref, x_ref, g_ref, o_ref = refs
        act = a_ref[0]
    else:
        a_ref, b_ref, gate_ref, ng_ref, w_ref, x_ref, g_ref, o_ref = refs
        ssum = a_ref[0] + b_ref[0]
        d = ssum.shape[1]
        hs = jnp.concatenate([_rms_rows(ssum[:, c0:c0 + hd]) for c0 in range(0, d, hd)], axis=1)
        hs = hs * ng_ref[...]
        gate = gate_ref[0]
        if mode == "mlstm":
            act = _sigmoid(gate) * hs
        else:
            act = hs * (gate * _sigmoid(gate))
    o_ref[0] = x_ref[0] + g_ref[0] * _mm(act.astype(BF16), w_ref[...])


def _outproj(mode, x, mod, w, a, b=None, gate_src=None, gate_block=0, norm_g=None, hd=0):
    bsz, s, d = x.shape
    tm = min(ROW_TILE, s)
    row = pl.BlockSpec((1, tm, d), lambda i, j: (i, j, 0))
    in_specs, args = [row], [a]
    if mode != "attn":
        in_specs += [row, pl.BlockSpec((1, tm, d), lambda i, j: (i, j, gate_block)),
                     pl.BlockSpec((1, d), lambda i, j: (0, 0))]
        args += [b, gate_src, norm_g.reshape(1, d)]
    in_specs += [pl.BlockSpec((d, d), lambda i, j: (0, 0)), row,
                 pl.BlockSpec((1, 1, d), lambda i, j: (i, 0, 2))]
    args += [w, x, mod]
    return pl.pallas_call(
        functools.partial(_outproj_kernel, mode=mode, hd=hd),
        out_shape=jax.ShapeDtypeStruct((bsz, s, d), F32),
        grid=(bsz, s // tm),
        in_specs=in_specs,
        out_specs=row,
        compiler_params=_params("parallel", "parallel"),
        name="outproj_" + mode,
    )(*args)


def _ffn_kernel(*refs, tm, dff, tn, final):
    if final:
        xp_ref, x_ref, xn_ref, sh_ref, sc_ref, g_ref, wu_ref, cw_ref, cb_ref, wd_ref, fg_ref, o_ref = refs
    else:
        xp_ref, x_ref, xn_ref, sh_ref, sc_ref, g_ref, wu_ref, cw_ref, cb_ref, wd_ref, o_ref = refs
    x = x_ref[0]
    ext = jnp.concatenate([xp_ref[0], x, xn_ref[0]], axis=0)
    hn = _rms_rows(ext) * (1.0 + sc_ref[0]) + sh_ref[0]
    hn = jnp.where(_halo_valid(tm), hn, 0.0).astype(BF16)
    acc = jnp.zeros_like(x)
    for c0 in range(0, dff, tn):
        ua = _dwconv_rows(_mm(hn, wu_ref[:, c0:c0 + tn]), cw_ref, c0, c0 + tn, FFN_CONV, tm)
        ug = _dwconv_rows(_mm(hn, wu_ref[:, dff + c0:dff + c0 + tn]), cw_ref, dff + c0, dff + c0 + tn,
                          FFN_CONV, tm)
        ua = ua + cb_ref[:, c0:c0 + tn]
        ug = ug + cb_ref[:, dff + c0:dff + c0 + tn]
        act = ua * (ug * _sigmoid(ug))
        acc = acc + _mm(act.astype(BF16), wd_ref[c0:c0 + tn, :])
    y = x + g_ref[0] * acc
    if final:
        y = _rms_rows(y) * fg_ref[...]
    o_ref[0] = y


def _conv_ffn(x, mod, w_up, conv_w, conv_b, w_down, final_g=None):
    b, s, d = x.shape
    dff = w_down.shape[0]
    tm = min(ROW_TILE, s)
    final = final_g is not None
    const = lambda i, j: (0, 0)
    in_specs = _seq_halo_specs(tm, s, d) + [
        pl.BlockSpec((1, 1, d), lambda i, j: (i, 0, 3)),
        pl.BlockSpec((1, 1, d), lambda i, j: (i, 0, 4)),
        pl.BlockSpec((1, 1, d), lambda i, j: (i, 0, 5)),
        pl.BlockSpec(w_up.shape, const),
        pl.BlockSpec(conv_w.shape, const),
        pl.BlockSpec((1, 2 * dff), const),
        pl.BlockSpec(w_down.shape, const)]
    args = [x, x, x, mod, mod, mod, w_up, conv_w, conv_b.reshape(1, 2 * dff), w_down]
    if final:
        in_specs.append(pl.BlockSpec((1, d), const))
        args.append(final_g.reshape(1, d))
    return pl.pallas_call(
        functools.partial(_ffn_kernel, tm=tm, dff=dff, tn=FFN_COLS, final=final),
        out_shape=jax.ShapeDtypeStruct((b, s, d), F32),
        grid=(b, s // tm),
        in_specs=in_specs,
        out_specs=pl.BlockSpec((1, tm, d), lambda i, j: (i, j, 0)),
        compiler_params=_params("parallel", "parallel"),
        name="conv_ffn",
    )(*args)


def _mlstm_layer(x, mod, w_in, gate_b, norm_g, w_out):
    heads = A_HEADS
    n_main = w_in.shape[1] - 4 * heads
    proj, gates = _inproj(x, mod, _split_w_in(w_in, n_main), n_main)
    hs = []
    for direction in range(2):
        lo = 2 * heads * direction
        gcol, grow = _gate_layouts(gates, lo, 2 * heads, min(MLSTM_CHUNK, x.shape[1]))
        bias = gate_b[lo:lo + 2 * heads].astype(F32)
        hs.append(_mlstm_scan(proj, gcol, grow, bias.reshape(1, -1), bias.reshape(-1, 1),
                              reverse=direction == 1))
    d = x.shape[2]
    return _outproj("mlstm", x, mod, w_out.astype(BF16), hs[0], hs[1], gate_src=proj,
                    gate_block=n_main // d - 1, norm_g=norm_g, hd=d // heads)


def _gdn_layer(x, mod, w_in, conv_w, a_log, dt_bias, norm_g, w_out):
    heads = B_HEADS
    n_main = w_in.shape[1] - 4 * heads
    proj, gates = _inproj(x, mod, _split_w_in(w_in, n_main), n_main)
    qkv = _gdn_prep(proj, conv_w)
    outs = []
    for direction in range(2):
        lo = 2 * heads * direction
        gcol = gates[:, :, lo:lo + 2 * heads]
        grow = jnp.swapaxes(gcol, 1, 2)
        p = jnp.stack([a_log[direction], dt_bias[direction]]).astype(F32)
        outs.append(_gdn_scan(qkv, gcol, grow, p, p.T, reverse=direction == 1))
    d = x.shape[2]
    return _outproj("gdn", x, mod, w_out.astype(BF16), outs[0], outs[1], gate_src=proj,
                    gate_block=n_main // d - 1, norm_g=jnp.tile(norm_g, heads), hd=d // heads)


def _attn_layer(x, mod, w_in, lam_p, norm_g, w_out, bias, lambda_init):
    proj = _inproj(x, mod, w_in.astype(BF16), w_in.shape[1])[0]
    o = _diff_attn(proj, bias, lam_p.astype(F32), norm_g, lambda_init)
    return _outproj("attn", x, mod, w_out.astype(BF16), o)


def _trunk(x, mods, bias_tiles, ada_w, ada_b, a_w_in, a_gate_b, a_norm_g, a_w_out, b_w_in, b_conv_w,
           b_a_log, b_dt_bias, b_norm_g, b_w_out, c_w_in, c_lambda, c_norm_g, c_w_out, rel_bias,
           ffn_w_up, ffn_conv_w, ffn_conv_b, ffn_w_down, final_g):
    depth = ada_w.shape[0]
    for i in range(depth):
        mod = mods[i][:, None, :]
        j = i // N_MIXERS
        kind = i % N_MIXERS
        if kind == 0:
            x = _mlstm_layer(x, mod, a_w_in[j], a_gate_b[j], a_norm_g[j], a_w_out[j])
        elif kind == 1:
            x = _gdn_layer(x, mod, b_w_in[j], b_conv_w[j], b_a_log[j], b_dt_bias[j], b_norm_g[j],
                           b_w_out[j])
        else:
            x = _attn_layer(x, mod, c_w_in[j], c_lambda[j], c_norm_g[j], c_w_out[j], bias_tiles,
                            0.8 - 0.6 * math.exp(-0.3 * i))
        x = _conv_ffn(x, mod, ffn_w_up[i].astype(BF16), ffn_conv_w[i], ffn_conv_b[i],
                      ffn_w_down[i].astype(BF16), final_g if i == depth - 1 else None)
    return x


def kernel(x_prompt, x_sample, c_prompt, c_sample, ada_w, ada_b, a_w_in, a_gate_b, a_norm_g, a_w_out, b_w_in, b_conv_w, b_a_log, b_dt_bias, b_norm_g, b_w_out, c_w_in, c_lambda, c_norm_g, c_w_out, rel_bias, ffn_w_up, ffn_conv_w, ffn_conv_b, ffn_w_down, final_g):
    weights = (ada_w, ada_b, a_w_in, a_gate_b, a_norm_g, a_w_out, b_w_in, b_conv_w, b_a_log,
               b_dt_bias, b_norm_g, b_w_out, c_w_in, c_lambda, c_norm_g, c_w_out, rel_bias,
               ffn_w_up, ffn_conv_w, ffn_conv_b, ffn_w_down, final_g)
    nb_p, nb_s = c_prompt.shape[0], c_sample.shape[0]
    c_all = jnp.concatenate([c_prompt, c_sample], axis=0)
    pad = (-c_all.shape[0]) % SUBLANES
    mods = _ada_mod(jnp.pad(c_all, ((0, pad), (0, 0))), ada_w, ada_b)
    outs, bias_tiles = [], {}
    for x, lo, n in ((x_prompt, 0, nb_p), (x_sample, nb_p, nb_s)):
        t = min(ATTN_TILE, x.shape[1])
        if t not in bias_tiles:
            bias_tiles[t] = _bias_tiles(rel_bias, t)
        outs.append(_trunk(x, mods[:, lo:lo + n], bias_tiles[t], *weights))
    return tuple(outs)
```

```python
import functools
import math

import jax
import jax.numpy as jnp
from jax import lax
from jax.experimental import pallas as pl
from jax.experimental.pallas import tpu as pltpu

F32 = jnp.float32
BF16 = jnp.bfloat16
EPS = 1e-6
LOG2E = math.log2(math.e)

N_MIXERS = 3
A_HEADS = 4
B_HEADS = 8
B_CONV = 5
C_HEADS = 8
REL_BUCKETS = 32
REL_MAX_DIST = 128
FFN_CONV = 3

LANES = 128
SUBLANES = 8
HALO = SUBLANES
VMEM_LIMIT = 56 * 1024 * 1024

ROW_TILE = 512
SCAN_BLOCK = 512
SCAN_CHUNK = 64
GDN_GROUP = 256
MLSTM_CHUNK = 256
ATTN_TILE = 512
ATTN_ROWS = 2048
ATTN_KEYS = 2048
FFN_COLS = 1408


def _params(*sem):
    return pltpu.CompilerParams(dimension_semantics=sem, vmem_limit_bytes=VMEM_LIMIT)


def _sigmoid(x):
    return 1.0 / (1.0 + jnp.exp(-x))


def _softplus(x):
    return jnp.maximum(x, 0.0) + jnp.log(1.0 + jnp.exp(-jnp.abs(x)))


def _rms_rows(x):
    return x * lax.rsqrt(jnp.mean(x * x, axis=-1, keepdims=True) + EPS)


def _mm(a, b):
    return jnp.dot(a, b, preferred_element_type=F32)


def _mm_inv(a, b):
    return _mm(a.astype(BF16), b.astype(BF16))


def _mm_nt(a, b):
    return lax.dot_general(a, b, (((1,), (1,)), ((), ())), preferred_element_type=F32)


def _mm_tn(a, b):
    return lax.dot_general(a, b, (((0,), (0,)), ((), ())), preferred_element_type=F32)


def _mod_kernel(c_ref, w_ref, b_ref, o_ref):
    c = c_ref[...]
    cs = c * _sigmoid(c)
    o_ref[0] = _mm(cs.astype(BF16), w_ref[0].astype(BF16)) + b_ref[0]


def _ada_mod(c, ada_w, ada_b):
    depth, d, n = ada_w.shape
    bp = c.shape[0]
    tn = 1536
    return pl.pallas_call(
        _mod_kernel,
        out_shape=jax.ShapeDtypeStruct((depth, bp, n), F32),
        grid=(depth, n // tn),
        in_specs=[pl.BlockSpec((bp, d), lambda l, j: (0, 0)),
                  pl.BlockSpec((1, d, tn), lambda l, j: (l, 0, j)),
                  pl.BlockSpec((1, 1, tn), lambda l, j: (l, 0, j))],
        out_specs=pl.BlockSpec((1, bp, tn), lambda l, j: (l, 0, j)),
        compiler_params=_params("parallel", "parallel"),
        name="ada_mod",
    )(c, ada_w, ada_b.reshape(depth, 1, n))


def _inproj_kernel(x_ref, sh_ref, sc_ref, w_ref, *o_refs, n_main, tn, lead_cols, lead_scale):
    hn = (_rms_rows(x_ref[0]) * (1.0 + sc_ref[0]) + sh_ref[0]).astype(BF16)
    for c0 in range(0, n_main, tn):
        y = _mm(hn, w_ref[:, c0:c0 + tn])
        if c0 < lead_cols:
            y = y * lead_scale
        o_refs[0][0, :, c0:c0 + tn] = y.astype(o_refs[0].dtype)
    if len(o_refs) > 1:
        o_refs[1][0] = _mm(hn, w_ref[:, n_main:])


def _inproj(x, mod, w, n_main, out_dtype=F32, lead_cols=0, lead_scale=1.0):
    b, s, d = x.shape
    n_all = w.shape[1]
    tm = min(ROW_TILE, s)
    tn = 512
    assert lead_cols % tn == 0
    out_shape = [jax.ShapeDtypeStruct((b, s, n_main), out_dtype)]
    out_specs = [pl.BlockSpec((1, tm, n_main), lambda i, j: (i, j, 0))]
    if n_all > n_main:
        out_shape.append(jax.ShapeDtypeStruct((b, s, n_all - n_main), F32))
        out_specs.append(pl.BlockSpec((1, tm, n_all - n_main), lambda i, j: (i, j, 0)))
    return pl.pallas_call(
        functools.partial(_inproj_kernel, n_main=n_main, tn=tn, lead_cols=lead_cols,
                          lead_scale=lead_scale),
        out_shape=out_shape,
        grid=(b, s // tm),
        in_specs=[pl.BlockSpec((1, tm, d), lambda i, j: (i, j, 0)),
                  pl.BlockSpec((1, 1, d), lambda i, j: (i, 0, 0)),
                  pl.BlockSpec((1, 1, d), lambda i, j: (i, 0, 1)),
                  pl.BlockSpec((d, n_all), lambda i, j: (0, 0))],
        out_specs=out_specs,
        compiler_params=_params("parallel", "parallel"),
        name="inproj",
    )(x, mod, mod, w)


def _split_w_in(w, n_main):
    d, n = w.shape
    if n == n_main:
        return w.astype(BF16)
    pad = LANES - (n - n_main)
    return jnp.pad(w, ((0, 0), (0, pad))).astype(BF16)


def _gate_layouts(g, lo, width, chunk):
    b, s, _ = g.shape
    col = g[:, :, lo:lo + width].reshape(b, s // chunk, chunk, width)
    return col, jnp.swapaxes(col, 2, 3)


def _mlstm_kernel(q_ref, k_ref, v_ref, gcol_ref, grow_ref, bcol_ref, brow_ref, h_ref,
                  c_s, n_s, m_s, *, reverse, chunk, heads, dqk, dv):
    @pl.when(pl.program_id(1) == 0)
    def _():
        c_s[...] = jnp.zeros_like(c_s)
        n_s[...] = jnp.zeros_like(n_s)
        m_s[...] = jnp.zeros_like(m_s)

    ncb = gcol_ref.shape[1]
    row = lax.broadcasted_iota(jnp.int32, (chunk, chunk), 0)
    col = lax.broadcasted_iota(jnp.int32, (chunk, chunk), 1)
    mask = (col >= row) if reverse else (col <= row)
    mask_t = (row >= col) if reverse else (row <= col)
    scale = dqk ** -0.5

    def step(ci, carry):
        cc = (ncb - 1 - ci) if reverse else ci
        r0 = pl.multiple_of(cc * chunk, chunk)
        gcol = gcol_ref[0, cc] + bcol_ref[...]
        grow = grow_ref[0, cc] + brow_ref[...]
        fcol = -_softplus(-gcol)
        frow = -_softplus(-grow)
        for h in range(heads):
            i_col = gcol[:, h:h + 1]
            f_col = fcol[:, heads + h:heads + h + 1]
            i_row = grow[h:h + 1, :]
            f_row = frow[heads + h:heads + h + 1, :]
            b_col = jnp.sum(jnp.where(mask, f_row, 0.0), axis=1, keepdims=True)
            b_row = jnp.sum(jnp.where(mask_t, f_col, 0.0), axis=0, keepdims=True)
            total = jnp.sum(f_row, axis=1, keepdims=True)
            m_prev = m_s[h]
            dmat = jnp.where(mask, b_col + (i_row - b_row), -jnp.inf)
            inter = b_col + m_prev
            m_row = jnp.maximum(inter, jnp.max(dmat, axis=1, keepdims=True))
            w_intra = jnp.exp(dmat - m_row)
            w_inter = jnp.exp(inter - m_row)
            q = q_ref[0, pl.ds(r0, chunk), h * dqk:(h + 1) * dqk] * scale
            k = k_ref[0, pl.ds(r0, chunk), h * dqk:(h + 1) * dqk]
            v = v_ref[0, pl.ds(r0, chunk), h * dv:(h + 1) * dv]
            c_prev = c_s[h]
            n_prev = n_s[h]
            qkc = _mm(q.astype(BF16),
                      jnp.concatenate([k.T.astype(BF16), c_prev.astype(BF16)], axis=1))
            sqk = qkc[:, :chunk] * w_intra
            dlast = total + (i_col - b_col)
            m_new = jnp.maximum(total + m_prev, jnp.max(dlast, axis=0, keepdims=True))
            kw = k * jnp.exp(dlast - m_new)
            both = _mm(jnp.concatenate([sqk.astype(BF16), kw.T.astype(BF16)], axis=0),
                       v.astype(BF16))
            num = w_inter * qkc[:, chunk:] + both[:chunk]
            den = (w_inter * jnp.sum(q * n_prev, axis=1, keepdims=True)
                   + jnp.sum(sqk, axis=1, keepdims=True))
            h_ref[0, pl.ds(r0, chunk), h * dv:(h + 1) * dv] = (
                num / jnp.maximum(jnp.abs(den), jnp.exp(-m_row)))
            decay = jnp.exp(total + m_prev - m_new)
            c_s[h] = decay * c_prev + both[chunk:]
            n_s[h] = decay * n_prev + jnp.sum(kw, axis=0, keepdims=True)
            m_s[h] = m_new
        return carry

    lax.fori_loop(0, ncb, step, 0)


def _mlstm_scan(proj, gcol, grow, bcol, brow, *, reverse):
    b, s, _ = proj.shape
    heads = A_HEADS
    dqk = proj.shape[2] // (6 * heads)
    dv = 2 * dqk
    nq, nv = heads * dqk, heads * dv
    blk = min(SCAN_BLOCK, s)
    nb = s // blk
    chunk = gcol.shape[2]
    ncb = blk // chunk
    ng = gcol.shape[3]

    def seq(j):
        return (nb - 1 - j) if reverse else j

    return pl.pallas_call(
        functools.partial(_mlstm_kernel, reverse=reverse, chunk=chunk, heads=heads,
                          dqk=dqk, dv=dv),
        out_shape=jax.ShapeDtypeStruct((b, s, nv), F32),
        grid=(b, nb),
        in_specs=[pl.BlockSpec((1, blk, nq), lambda i, j: (i, seq(j), 0)),
                  pl.BlockSpec((1, blk, nq), lambda i, j: (i, seq(j), 1)),
                  pl.BlockSpec((1, blk, nv), lambda i, j: (i, seq(j), 1)),
                  pl.BlockSpec((1, ncb, chunk, ng), lambda i, j: (i, seq(j), 0, 0)),
                  pl.BlockSpec((1, ncb, ng, chunk), lambda i, j: (i, seq(j), 0, 0)),
                  pl.BlockSpec((1, ng), lambda i, j: (0, 0)),
                  pl.BlockSpec((ng, 1), lambda i, j: (0, 0))],
        out_specs=pl.BlockSpec((1, blk, nv), lambda i, j: (i, seq(j), 0)),
        scratch_shapes=[pltpu.VMEM((heads, dqk, dv), F32),
                        pltpu.VMEM((heads, 1, dqk), F32),
                        pltpu.VMEM((heads, 1, 1), F32)],
        compiler_params=_params("parallel", "arbitrary"),
        name="mlstm_scan_bwd" if reverse else "mlstm_scan_fwd",
    )(proj, proj, proj, gcol, grow, bcol, brow)


def _seq_halo_specs(tm, s, width, col_block=0):
    r = tm // HALO
    last = s // HALO - 1

    def prev(i, j):
        return (i, jnp.maximum(j * r - 1, 0), col_block)

    def nxt(i, j):
        return (i, jnp.minimum((j + 1) * r, last), col_block)

    return [pl.BlockSpec((1, HALO, width), prev),
            pl.BlockSpec((1, tm, width), lambda i, j: (i, j, col_block)),
            pl.BlockSpec((1, HALO, width), nxt)]


def _halo_valid(tm):
    j = pl.program_id(1)
    rid = lax.broadcasted_iota(jnp.int32, (tm + 2 * HALO, 1), 0)
    return jnp.logical_and(jnp.logical_or(rid >= HALO, j > 0),
                           jnp.logical_or(rid < tm + HALO, j < pl.num_programs(1) - 1))


def _dwconv_rows(ext, w_ref, c0, c1, taps, tm):
    n = ext.shape[0]
    acc = None
    for t in range(taps):
        off = t - taps // 2
        src = ext if off == 0 else pltpu.roll(ext, (-off) % n, 0)
        term = src * w_ref[t:t + 1, c0:c1]
        acc = term if acc is None else acc + term
    return acc[HALO:HALO + tm]


def _gdn_prep(proj, conv_w):
    b, s, n = proj.shape
    heads = B_HEADS
    dk = n // (4 * heads)
    width = 3 * heads * dk
    tm = min(ROW_TILE, s)
    specs = []
    for cb in range(3):
        specs += _seq_halo_specs(tm, s, heads * dk, cb)
    return pl.pallas_call(
        functools.partial(_gdn_prep_kernel, tm=tm, heads=heads, dk=dk),
        out_shape=jax.ShapeDtypeStruct((b, s, width), F32),
        grid=(b, s // tm),
        in_specs=specs + [pl.BlockSpec((B_CONV, width), lambda i, j: (0, 0))],
        out_specs=pl.BlockSpec((1, tm, width), lambda i, j: (i, j, 0)),
        compiler_params=_params("parallel", "parallel"),
        name="gdn_prep",
    )(*([proj] * 9), conv_w)


def _gdn_prep_kernel(qp, q, qn, kp, k, kn, vp, v, vn, w_ref, o_ref, *, tm, heads, dk):
    valid = _halo_valid(tm)
    hw = heads * dk
    for part, (p_ref, x_ref, n_ref) in enumerate(((qp, q, qn), (kp, k, kn), (vp, v, vn))):
        for h in range(heads):
            c0, c1 = h * dk, (h + 1) * dk
            ext = jnp.concatenate([p_ref[0, :, c0:c1], x_ref[0, :, c0:c1], n_ref[0, :, c0:c1]], axis=0)
            ext = jnp.where(valid, ext, 0.0)
            y = _dwconv_rows(ext, w_ref, part * hw + c0, part * hw + c1, B_CONV, tm)
            y = y * _sigmoid(y)
            if part < 2:
                y = y * lax.rsqrt(jnp.sum(y * y, axis=-1, keepdims=True) + EPS)
            if part == 0:
                y = y * (dk ** -0.5)
            o_ref[0, :, part * hw + c0:part * hw + c1] = y


def _unit_tri_inverse(a, eye, same_base, level_masks):
    mm = _mm_inv
    n1 = jnp.where(same_base, a, 0.0)
    n2 = mm(n1, n1)
    n4 = mm(n2, n2)
    x = mm(mm(eye - n1, eye + n2), eye + n4)
    for lm in level_masks:
        x = x - mm(mm(x, jnp.where(lm, a, 0.0)), x)
    return x


def _gdn_kernel(q_ref, k_ref, v_ref, gcol_ref, grow_ref, pcol_ref, prow_ref, o_ref,
                s_s, mq_s, r_s, gl_s, *, reverse, chunk, group, heads, dk):
    @pl.when(pl.program_id(1) == 0)
    def _():
        s_s[...] = jnp.zeros_like(s_s)

    blk = q_ref.shape[1]
    cpg = group // chunk
    clog = chunk.bit_length() - 1
    row = lax.broadcasted_iota(jnp.int32, (group, group), 0)
    col = lax.broadcasted_iota(jnp.int32, (group, group), 1)

    def same_block(log2):
        return jnp.right_shift(row, log2) == jnp.right_shift(col, log2)

    same_chunk = same_block(clog)
    causal = jnp.logical_and(same_chunk, (col >= row) if reverse else (col <= row))
    causal_t = jnp.logical_and(same_chunk, (row >= col) if reverse else (row <= col))
    strict = jnp.logical_and(same_chunk, (col > row) if reverse else (col < row))
    eye = (row == col).astype(F32)
    base_log2 = 3
    same_base = same_block(base_log2)
    level_masks = [jnp.logical_and(same_block(lg + 1), jnp.logical_not(same_block(lg)))
                   for lg in range(base_log2, clog)]
    chunk_of_row = jnp.right_shift(lax.broadcasted_iota(jnp.int32, (group, 1), 0), clog)

    raw_col = gcol_ref[0]
    raw_row = grow_ref[0]
    g_cols = -jnp.exp(pcol_ref[0:1, :]) * _softplus(raw_col[:, :heads] + pcol_ref[1:2, :])
    g_rows = -jnp.exp(prow_ref[:, 0:1]) * _softplus(raw_row[:heads, :] + prow_ref[:, 1:2])
    beta_cols = _sigmoid(raw_col[:, heads:])

    for h in range(heads):
        for gi in range(blk // group):
            r0 = gi * group
            g_col = g_cols[r0:r0 + group, h:h + 1]
            g_row = g_rows[h:h + 1, r0:r0 + group]
            beta = beta_cols[r0:r0 + group, h:h + 1]
            gc_col = jnp.sum(jnp.where(causal, g_row, 0.0), axis=1, keepdims=True)
            gc_row = jnp.sum(jnp.where(causal_t, g_col, 0.0), axis=0, keepdims=True)
            tot_col = jnp.sum(jnp.where(same_chunk, g_row, 0.0), axis=1, keepdims=True)
            decay = jnp.exp(jnp.where(causal, gc_col - gc_row, -jnp.inf))
            q = q_ref[0, r0:r0 + group, h * dk:(h + 1) * dk]
            k = k_ref[0, r0:r0 + group, h * dk:(h + 1) * dk]
            v = v_ref[0, r0:r0 + group, h * dk:(h + 1) * dk]
            kb = k.astype(BF16)
            kq = _mm_nt(jnp.concatenate([kb, q.astype(BF16)], axis=0), kb)
            a = jnp.where(strict, kq[:group] * decay, 0.0) * beta
            t_inv = _unit_tri_inverse(a, eye, same_base, level_masks)
            eg = jnp.exp(gc_col)
            kbeta = k * beta
            wu = _mm_inv(t_inv, jnp.concatenate([kbeta * eg, v * beta], axis=1)).astype(BF16)
            qo = _mm((kq[group:] * decay).astype(BF16), wu)
            o_ref[0, r0:r0 + group, h * dk:(h + 1) * dk] = qo[:, dk:]
            q_prime = (q * eg - qo[:, :dk]).astype(BF16)
            kd = k * jnp.exp(tot_col - gc_col)
            kd_by_chunk = jnp.concatenate(
                [jnp.where(chunk_of_row == c, kd, 0.0) for c in range(cpg)], axis=1).astype(BF16)
            mr = _mm_tn(kd_by_chunk, wu)
            for c in range(cpg):
                cg = gi * cpg + c
                mq_s[h, cg, 0:dk, :] = mr[c * dk:(c + 1) * dk, :dk].astype(BF16)
                mq_s[h, cg, dk:dk + chunk, :] = q_prime[c * chunk:(c + 1) * chunk]
                r_s[h, cg] = mr[c * dk:(c + 1) * dk, dk:]
                total = jnp.sum(g_row[:, c * chunk:(c + 1) * chunk], axis=1, keepdims=True)
                gl_s[h, cg] = jnp.broadcast_to(jnp.exp(total), (1, dk))

    ncb = blk // chunk
    for ci in range(ncb):
        cc = (ncb - 1 - ci) if reverse else ci
        for h in range(heads):
            s_prev = s_s[h]
            p = _mm(mq_s[h, cc], s_prev.astype(BF16))
            s_s[h] = gl_s[h, cc] * s_prev - p[:dk] + r_s[h, cc]
            o_ref[0, cc * chunk:(cc + 1) * chunk, h * dk:(h + 1) * dk] += p[dk:]


def _gdn_scan(qkv, gcol, grow, pcol, prow, *, reverse):
    b, s, n = qkv.shape
    heads = B_HEADS
    hw = n // 3
    dk = hw // heads
    blk = min(SCAN_BLOCK, s)
    nb = s // blk
    ncb = blk // SCAN_CHUNK
    ng = gcol.shape[2]

    def seq(j):
        return (nb - 1 - j) if reverse else j

    return pl.pallas_call(
        functools.partial(_gdn_kernel, reverse=reverse, chunk=SCAN_CHUNK, group=GDN_GROUP,
                          heads=heads, dk=dk),
        out_shape=jax.ShapeDtypeStruct((b, s, hw), F32),
        grid=(b, nb),
        in_specs=[pl.BlockSpec((1, blk, hw), lambda i, j: (i, seq(j), 0)),
                  pl.BlockSpec((1, blk, hw), lambda i, j: (i, seq(j), 1)),
                  pl.BlockSpec((1, blk, hw), lambda i, j: (i, seq(j), 2)),
                  pl.BlockSpec((1, blk, ng), lambda i, j: (i, seq(j), 0)),
                  pl.BlockSpec((1, ng, blk), lambda i, j: (i, 0, seq(j))),
                  pl.BlockSpec((2, heads), lambda i, j: (0, 0)),
                  pl.BlockSpec((heads, 2), lambda i, j: (0, 0))],
        out_specs=pl.BlockSpec((1, blk, hw), lambda i, j: (i, seq(j), 0)),
        scratch_shapes=[pltpu.VMEM((heads, dk, dk), F32),
                        pltpu.VMEM((heads, ncb, dk + SCAN_CHUNK, dk), BF16),
                        pltpu.VMEM((heads, ncb, dk, dk), F32),
                        pltpu.VMEM((heads, ncb, 1, dk), F32)],
        compiler_params=_params("parallel", "arbitrary"),
        name="gdn_scan_bwd" if reverse else "gdn_scan_fwd",
    )(qkv, qkv, qkv, gcol, grow, pcol, prow)


def _rel_bucket(rel):
    nb = REL_BUCKETS // 2
    exact = nb // 2
    n = jnp.abs(rel)
    large = exact + (jnp.log(jnp.maximum(n, 1).astype(jnp.float32) / exact)
                     / math.log(REL_MAX_DIST / exact) * (nb - exact)).astype(jnp.int32)
    large = jnp.minimum(large, nb - 1)
    return jnp.where(rel > 0, nb, 0) + jnp.where(n < exact, n, large)


def _bias_tiles(rel_bias, t):
    assert t >= REL_MAX_DIST
    n = 2 * t + 1
    m = jnp.arange(n, dtype=jnp.int32)
    rel_in_tile = jnp.where(m < t, m, m - n)
    d = jnp.arange(-2, 3, dtype=jnp.int32)
    rel = d[:, None] * t + rel_in_tile[None, :]
    vals = jnp.transpose(rel_bias.astype(F32)[_rel_bucket(rel)], (2, 0, 1)) * LOG2E
    h = vals.shape[0]
    tiled = jnp.tile(vals, (1, 1, t))[:, :, :t * (n - 1)].reshape(h, 5, t, n - 1)
    return tiled[:, :, :, :t]


def _attn_kernel(q_ref, k_ref, v_ref, bias_ref, lam_ref, ng_ref, o_ref,
                 m1_s, a1_s, m2_s, a2_s, vext_s, *, lambda_init, dh):
    qi = pl.program_id(2)
    ki = pl.program_id(3)
    t = bias_ref.shape[2]
    nq = q_ref.shape[1] // t
    tk, dv = v_ref.shape[1], v_ref.shape[2]
    r = tk // t
    maps = ((m1_s, a1_s), (m2_s, a2_s))

    @pl.when(ki == 0)
    def _():
        for m_s, a_s in maps:
            m_s[...] = jnp.full_like(m_s, -jnp.inf)
            a_s[...] = jnp.zeros_like(a_s)
        vext_s[:, dv:] = jnp.ones((tk, dv), BF16)

    vext_s[:, :dv] = v_ref[0]
    k = k_ref[0]
    first = ki * r - qi * nq

    def update(with_bias):
        for iq in range(nq):
            rows = slice(iq * t, (iq + 1) * t)
            off = first - iq
            if with_bias:
                bias = jnp.concatenate(
                    [bias_ref[0, jnp.clip(off + j, -2, 2) + 2] for j in range(r)], axis=1)
                shift = 0.0
            else:
                shift = bias_ref[0, jnp.clip(off, -2, 2) + 2, 0:1, 0:1]
            for idx, (m_s, a_s) in enumerate(maps):
                s = _mm_nt(q_ref[0, rows, idx * dh:(idx + 1) * dh], k[:, idx * dh:(idx + 1) * dh])
                if with_bias:
                    s = s + bias
                m_prev = m_s[rows]
                m_new = jnp.maximum(m_prev, jnp.max(s, axis=1, keepdims=True) + shift)
                p = jnp.exp2(s - (m_new - shift))
                a_s[rows] = (jnp.exp2(m_prev - m_new) * a_s[rows]
                             + _mm(p.astype(BF16), vext_s[...]))
                m_s[rows] = m_new

    far = jnp.logical_or(first - (nq - 1) >= 2, first + (r - 1) <= -2)

    @pl.when(far)
    def _():
        update(False)

    @pl.when(jnp.logical_not(far))
    def _():
        update(True)

    @pl.when(ki == pl.num_programs(3) - 1)
    def _():
        lp = lam_ref[...]
        lam = (jnp.exp(jnp.sum(lp[0:1] * lp[1:2], axis=1, keepdims=True))
               - jnp.exp(jnp.sum(lp[2:3] * lp[3:4], axis=1, keepdims=True)) + lambda_init)
        a1, a2 = a1_s[...], a2_s[...]
        o = a1[:, :dv] / a1[:, dv:dv + 1] - lam * (a2[:, :dv] / a2[:, dv:dv + 1])
        o_ref[0] = _rms_rows(o) * ng_ref[...] * (1.0 - lambda_init)


def _diff_attn(proj, bias, lam_p, norm_g, lambda_init):
    b, s, n = proj.shape
    heads = C_HEADS
    dv = n // (3 * heads)
    t = bias.shape[2]
    tk = min(ATTN_KEYS, s)
    tq = min(ATTN_ROWS, s)
    return pl.pallas_call(
        functools.partial(_attn_kernel, lambda_init=lambda_init, dh=dv // 2),
        out_shape=jax.ShapeDtypeStruct((b, s, heads * dv), F32),
        grid=(b, heads, s // tq, s // tk),
        in_specs=[pl.BlockSpec((1, tq, dv), lambda i, h, qi, ki: (i, qi, h)),
                  pl.BlockSpec((1, tk, dv), lambda i, h, qi, ki: (i, ki, heads + h)),
                  pl.BlockSpec((1, tk, dv), lambda i, h, qi, ki: (i, ki, 2 * heads + h)),
                  pl.BlockSpec((1, 5, t, t), lambda i, h, qi, ki: (h, 0, 0, 0)),
                  pl.BlockSpec(lam_p.shape, lambda i, h, qi, ki: (0, 0)),
                  pl.BlockSpec((1, dv), lambda i, h, qi, ki: (0, 0))],
        out_specs=pl.BlockSpec((1, tq, dv), lambda i, h, qi, ki: (i, qi, h)),
        scratch_shapes=[pltpu.VMEM((tq, 1), F32), pltpu.VMEM((tq, 2 * dv), F32),
                        pltpu.VMEM((tq, 1), F32), pltpu.VMEM((tq, 2 * dv), F32),
                        pltpu.VMEM((tk, 2 * dv), BF16)],
        compiler_params=_params("parallel", "parallel", "parallel", "arbitrary"),
        name="diff_attn",
    )(proj, proj, proj, bias, lam_p, norm_g.reshape(1, dv))


def _outproj_kernel(*refs, mode, hd):
    if mode == "attn":
        a_ref, w_ref, x_ref, g_ref, o_ref = refs
        act = a_ref[0]
    else:
        a_ref, b_ref, gate_ref, ng_ref, w_ref, x_ref, g_ref, o_ref = refs
        ssum = a_ref[0] + b_ref[0]
        d = ssum.shape[1]
        hs = jnp.concatenate([_rms_rows(ssum[:, c0:c0 + hd]) for c0 in range(0, d, hd)], axis=1)
        hs = hs * ng_ref[...]
        gate = gate_ref[0]
        if mode == "mlstm":
            act = _sigmoid(gate) * hs
        else:
            act = hs * (gate * _sigmoid(gate))
    o_ref[0] = x_ref[0] + g_ref[0] * _mm(act.astype(BF16), w_ref[...])


def _outproj(mode, x, mod, w, a, b=None, gate_src=None, gate_block=0, norm_g=None, hd=0):
    bsz, s, d = x.shape
    tm = min(ROW_TILE, s)
    row = pl.BlockSpec((1, tm, d), lambda i, j: (i, j, 0))
    in_specs, args = [row], [a]
    if mode != "attn":
        in_specs += [row, pl.BlockSpec((1, tm, d), lambda i, j: (i, j, gate_block)),
                     pl.BlockSpec((1, d), lambda i, j: (0, 0))]
        args += [b, gate_src, norm_g.reshape(1, d)]
    in_specs += [pl.BlockSpec((d, d), lambda i, j: (0, 0)), row,
                 pl.BlockSpec((1, 1, d), lambda i, j: (i, 0, 2))]
    args += [w, x, mod]
    return pl.pallas_call(
        functools.partial(_outproj_kernel, mode=mode, hd=hd),
        out_shape=jax.ShapeDtypeStruct((bsz, s, d), F32),
        grid=(bsz, s // tm),
        in_specs=in_specs,
        out_specs=row,
        compiler_params=_params("parallel", "parallel"),
        name="outproj_" + mode,
    )(*args)


def _ffn_kernel(*refs, tm, dff, tn, final):
    if final:
        xp_ref, x_ref, xn_ref, sh_ref, sc_ref, g_ref, wu_ref, cw_ref, cb_ref, wd_ref, fg_ref, o_ref = refs
    else:
        xp_ref, x_ref, xn_ref, sh_ref, sc_ref, g_ref, wu_ref, cw_ref, cb_ref, wd_ref, o_ref = refs
    x = x_ref[0]
    ext = jnp.concatenate([xp_ref[0], x, xn_ref[0]], axis=0)
    hn = _rms_rows(ext) * (1.0 + sc_ref[0]) + sh_ref[0]
    hn = jnp.where(_halo_valid(tm), hn, 0.0).astype(BF16)
    acc = jnp.zeros_like(x)
    for c0 in range(0, dff, tn):
        ua = _dwconv_rows(_mm(hn, wu_ref[:, c0:c0 + tn]), cw_ref, c0, c0 + tn, FFN_CONV, tm)
        ug = _dwconv_rows(_mm(hn, wu_ref[:, dff + c0:dff + c0 + tn]), cw_ref, dff + c0, dff + c0 + tn,
                          FFN_CONV, tm)
        ua = ua + cb_ref[:, c0:c0 + tn]
        ug = ug + cb_ref[:, dff + c0:dff + c0 + tn]
        act = ua * (ug * _sigmoid(ug))
        acc = acc + _mm(act.astype(BF16), wd_ref[c0:c0 + tn, :])
    y = x + g_ref[0] * acc
    if final:
        y = _rms_rows(y) * fg_ref[...]
    o_ref[0] = y


def _conv_ffn(x, mod, w_up, conv_w, conv_b, w_down, final_g=None):
    b, s, d = x.shape
    dff = w_down.shape[0]
    tm = min(ROW_TILE, s)
    final = final_g is not None
    const = lambda i, j: (0, 0)
    in_specs = _seq_halo_specs(tm, s, d) + [
        pl.BlockSpec((1, 1, d), lambda i, j: (i, 0, 3)),
        pl.BlockSpec((1, 1, d), lambda i, j: (i, 0, 4)),
        pl.BlockSpec((1, 1, d), lambda i, j: (i, 0, 5)),
        pl.BlockSpec(w_up.shape, const),
        pl.BlockSpec(conv_w.shape, const),
        pl.BlockSpec((1, 2 * dff), const),
        pl.BlockSpec(w_down.shape, const)]
    args = [x, x, x, mod, mod, mod, w_up, conv_w, conv_b.reshape(1, 2 * dff), w_down]
    if final:
        in_specs.append(pl.BlockSpec((1, d), const))
        args.append(final_g.reshape(1, d))
    return pl.pallas_call(
        functools.partial(_ffn_kernel, tm=tm, dff=dff, tn=FFN_COLS, final=final),
        out_shape=jax.ShapeDtypeStruct((b, s, d), F32),
        grid=(b, s // tm),
        in_specs=in_specs,
        out_specs=pl.BlockSpec((1, tm, d), lambda i, j: (i, j, 0)),
        compiler_params=_params("parallel", "parallel"),
        name="conv_ffn",
    )(*args)


def _mlstm_layer(x, mod, w_in, gate_b, norm_g, w_out):
    heads = A_HEADS
    n_main = w_in.shape[1] - 4 * heads
    proj, gates = _inproj(x, mod, _split_w_in(w_in, n_main), n_main)
    hs = []
    for direction in range(2):
        lo = 2 * heads * direction
        gcol, grow = _gate_layouts(gates, lo, 2 * heads, min(MLSTM_CHUNK, x.shape[1]))
        bias = gate_b[lo:lo + 2 * heads].astype(F32)
        hs.append(_mlstm_scan(proj, gcol, grow, bias.reshape(1, -1), bias.reshape(-1, 1),
                              reverse=direction == 1))
    d = x.shape[2]
    return _outproj("mlstm", x, mod, w_out.astype(BF16), hs[0], hs[1], gate_src=proj,
                    gate_block=n_main // d - 1, norm_g=norm_g, hd=d // heads)


def _gdn_layer(x, mod, w_in, conv_w, a_log, dt_bias, norm_g, w_out):
    heads = B_HEADS
    n_main = w_in.shape[1] - 4 * heads
    proj, gates = _inproj(x, mod, _split_w_in(w_in, n_main), n_main)
    qkv = _gdn_prep(proj, conv_w)
    outs = []
    for direction in range(2):
        lo = 2 * heads * direction
        gcol = gates[:, :, lo:lo + 2 * heads]
        grow = jnp.swapaxes(gcol, 1, 2)
        p = jnp.stack([a_log[direction], dt_bias[direction]]).astype(F32)
        outs.append(_gdn_scan(qkv, gcol, grow, p, p.T, reverse=direction == 1))
    d = x.shape[2]
    return _outproj("gdn", x, mod, w_out.astype(BF16), outs[0], outs[1], gate_src=proj,
                    gate_block=n_main // d - 1, norm_g=jnp.tile(norm_g, heads), hd=d // heads)


def _attn_layer(x, mod, w_in, lam_p, norm_g, w_out, bias, lambda_init):
    n = w_in.shape[1]
    dh = n // (6 * C_HEADS)
    proj = _inproj(x, mod, w_in.astype(BF16), n, out_dtype=BF16, lead_cols=n // 3,
                   lead_scale=dh ** -0.5 * LOG2E)[0]
    o = _diff_attn(proj, bias, lam_p.astype(F32), norm_g, lambda_init)
    return _outproj("attn", x, mod, w_out.astype(BF16), o)


def _trunk(x, mods, bias_tiles, ada_w, ada_b, a_w_in, a_gate_b, a_norm_g, a_w_out, b_w_in, b_conv_w,
           b_a_log, b_dt_bias, b_norm_g, b_w_out, c_w_in, c_lambda, c_norm_g, c_w_out, rel_bias,
           ffn_w_up, ffn_conv_w, ffn_conv_b, ffn_w_down, final_g):
    depth = ada_w.shape[0]
    for i in range(depth):
        mod = mods[i][:, None, :]
        j = i // N_MIXERS
        kind = i % N_MIXERS
        if kind == 0:
            x = _mlstm_layer(x, mod, a_w_in[j], a_gate_b[j], a_norm_g[j], a_w_out[j])
        elif kind == 1:
            x = _gdn_layer(x, mod, b_w_in[j], b_conv_w[j], b_a_log[j], b_dt_bias[j], b_norm_g[j],
                           b_w_out[j])
        else:
            x = _attn_layer(x, mod, c_w_in[j], c_lambda[j], c_norm_g[j], c_w_out[j], bias_tiles,
                            0.8 - 0.6 * math.exp(-0.3 * i))
        x = _conv_ffn(x, mod, ffn_w_up[i].astype(BF16), ffn_conv_w[i], ffn_conv_b[i],
                      ffn_w_down[i].astype(BF16), final_g if i == depth - 1 else None)
    return x


def kernel(x_prompt, x_sample, c_prompt, c_sample, ada_w, ada_b, a_w_in, a_gate_b, a_norm_g, a_w_out, b_w_in, b_conv_w, b_a_log, b_dt_bias, b_norm_g, b_w_out, c_w_in, c_lambda, c_norm_g, c_w_out, rel_bias, ffn_w_up, ffn_conv_w, ffn_conv_b, ffn_w_down, final_g):
    weights = (ada_w, ada_b, a_w_in, a_gate_b, a_norm_g, a_w_out, b_w_in, b_conv_w, b_a_log,
               b_dt_bias, b_norm_g, b_w_out, c_w_in, c_lambda, c_norm_g, c_w_out, rel_bias,
               ffn_w_up, ffn_conv_w, ffn_conv_b, ffn_w_down, final_g)
    nb_p, nb_s = c_prompt.shape[0], c_sample.shape[0]
    c_all = jnp.concatenate([c_prompt, c_sample], axis=0)
    pad = (-c_all.shape[0]) % SUBLANES
    mods = _ada_mod(jnp.pad(c_all, ((0, pad), (0, 0))), ada_w, ada_b)
    outs, bias_tiles = [], {}
    for x, lo, n in ((x_prompt, 0, nb_p), (x_sample, nb_p, nb_s)):
        t = min(ATTN_TILE, x.shape[1])
        if t not in bias_tiles:
            bias_tiles[t] = _bias_tiles(rel_bias, t)
        outs.append(_trunk(x, mods[:, lo:lo + n], bias_tiles[t], *weights))
    return tuple(outs)
```

```python
import functools
import math

import jax
import jax.numpy as jnp
from jax import lax
from jax.experimental import pallas as pl
from jax.experimental.pallas import tpu as pltpu

F32 = jnp.float32
BF16 = jnp.bfloat16
EPS = 1e-6
LOG2E = math.log2(math.e)

N_MIXERS = 3
A_HEADS = 4
B_HEADS = 8
B_CONV = 5
C_HEADS = 8
REL_BUCKETS = 32
REL_MAX_DIST = 128
FFN_CONV = 3

LANES = 128
SUBLANES = 8
HALO = SUBLANES
VMEM_LIMIT = 56 * 1024 * 1024

ROW_TILE = 512
SCAN_BLOCK = 512
SCAN_CHUNK = 64
GDN_GROUP = 256
GDN_UNITS = 4
MLSTM_CHUNK = 256
ATTN_TILE = 512
ATTN_ROWS = 2048
ATTN_KEYS = 2048
FFN_COLS = 1408


def _params(*sem):
    return pltpu.CompilerParams(dimension_semantics=sem, vmem_limit_bytes=VMEM_LIMIT)


def _sigmoid(x):
    return 1.0 / (1.0 + jnp.exp(-x))


def _softplus(x):
    return jnp.maximum(x, 0.0) + jnp.log(1.0 + jnp.exp(-jnp.abs(x)))


def _rms_rows(x):
    return x * lax.rsqrt(jnp.mean(x * x, axis=-1, keepdims=True) + EPS)


def _mm(a, b):
    return jnp.dot(a, b, preferred_element_type=F32)


def _mm_inv(a, b):
    return _mm(a.astype(BF16), b.astype(BF16))


def _mm_nt(a, b):
    return lax.dot_general(a, b, (((1,), (1,)), ((), ())), preferred_element_type=F32)


def _mm_tn(a, b):
    return lax.dot_general(a, b, (((0,), (0,)), ((), ())), preferred_element_type=F32)


def _mod_kernel(c_ref, w_ref, b_ref, o_ref):
    c = c_ref[...]
    cs = c * _sigmoid(c)
    o_ref[0] = _mm(cs.astype(BF16), w_ref[0].astype(BF16)) + b_ref[0]


def _ada_mod(c, ada_w, ada_b):
    depth, d, n = ada_w.shape
    bp = c.shape[0]
    tn = 1536
    return pl.pallas_call(
        _mod_kernel,
        out_shape=jax.ShapeDtypeStruct((depth, bp, n), F32),
        grid=(depth, n // tn),
        in_specs=[pl.BlockSpec((bp, d), lambda l, j: (0, 0)),
                  pl.BlockSpec((1, d, tn), lambda l, j: (l, 0, j)),
                  pl.BlockSpec((1, 1, tn), lambda l, j: (l, 0, j))],
        out_specs=pl.BlockSpec((1, bp, tn), lambda l, j: (l, 0, j)),
        compiler_params=_params("parallel", "parallel"),
        name="ada_mod",
    )(c, ada_w, ada_b.reshape(depth, 1, n))


def _inproj_kernel(x_ref, sh_ref, sc_ref, w_ref, *o_refs, n_main, tn, lead_cols, lead_scale):
    hn = (_rms_rows(x_ref[0]) * (1.0 + sc_ref[0]) + sh_ref[0]).astype(BF16)
    for c0 in range(0, n_main, tn):
        y = _mm(hn, w_ref[:, c0:c0 + tn])
        if c0 < lead_cols:
            y = y * lead_scale
        o_refs[0][0, :, c0:c0 + tn] = y.astype(o_refs[0].dtype)
    if len(o_refs) > 1:
        o_refs[1][0] = _mm(hn, w_ref[:, n_main:])


def _inproj(x, mod, w, n_main, out_dtype=F32, lead_cols=0, lead_scale=1.0):
    b, s, d = x.shape
    n_all = w.shape[1]
    tm = min(ROW_TILE, s)
    tn = 512
    assert lead_cols % tn == 0
    out_shape = [jax.ShapeDtypeStruct((b, s, n_main), out_dtype)]
    out_specs = [pl.BlockSpec((1, tm, n_main), lambda i, j: (i, j, 0))]
    if n_all > n_main:
        out_shape.append(jax.ShapeDtypeStruct((b, s, n_all - n_main), F32))
        out_specs.append(pl.BlockSpec((1, tm, n_all - n_main), lambda i, j: (i, j, 0)))
    return pl.pallas_call(
        functools.partial(_inproj_kernel, n_main=n_main, tn=tn, lead_cols=lead_cols,
                          lead_scale=lead_scale),
        out_shape=out_shape,
        grid=(b, s // tm),
        in_specs=[pl.BlockSpec((1, tm, d), lambda i, j: (i, j, 0)),
                  pl.BlockSpec((1, 1, d), lambda i, j: (i, 0, 0)),
                  pl.BlockSpec((1, 1, d), lambda i, j: (i, 0, 1)),
                  pl.BlockSpec((d, n_all), lambda i, j: (0, 0))],
        out_specs=out_specs,
        compiler_params=_params("parallel", "parallel"),
        name="inproj",
    )(x, mod, mod, w)


def _split_w_in(w, n_main):
    d, n = w.shape
    if n == n_main:
        return w.astype(BF16)
    pad = LANES - (n - n_main)
    return jnp.pad(w, ((0, 0), (0, pad))).astype(BF16)


def _gate_layouts(g, lo, width, chunk):
    b, s, _ = g.shape
    col = g[:, :, lo:lo + width].reshape(b, s // chunk, chunk, width)
    return col, jnp.swapaxes(col, 2, 3)


def _mlstm_kernel(q_ref, k_ref, v_ref, gcol_ref, grow_ref, bcol_ref, brow_ref, h_ref,
                  c_s, n_s, m_s, *, reverse, chunk, heads, dqk, dv):
    @pl.when(pl.program_id(1) == 0)
    def _():
        c_s[...] = jnp.zeros_like(c_s)
        n_s[...] = jnp.zeros_like(n_s)
        m_s[...] = jnp.zeros_like(m_s)

    ncb = gcol_ref.shape[1]
    row = lax.broadcasted_iota(jnp.int32, (chunk, chunk), 0)
    col = lax.broadcasted_iota(jnp.int32, (chunk, chunk), 1)
    mask = (col >= row) if reverse else (col <= row)
    mask_t = (row >= col) if reverse else (row <= col)
    scale = dqk ** -0.5

    def step(ci, carry):
        cc = (ncb - 1 - ci) if reverse else ci
        r0 = pl.multiple_of(cc * chunk, chunk)
        gcol = gcol_ref[0, cc] + bcol_ref[...]
        grow = grow_ref[0, cc] + brow_ref[...]
        fcol = -_softplus(-gcol)
        frow = -_softplus(-grow)
        hs = range(heads)
        i_col = [gcol[:, h:h + 1] for h in hs]
        f_col = [fcol[:, heads + h:heads + h + 1] for h in hs]
        i_row = [grow[h:h + 1, :] for h in hs]
        f_row = [frow[heads + h:heads + h + 1, :] for h in hs]
        b_col = [jnp.sum(jnp.where(mask, f_row[h], 0.0), axis=1, keepdims=True) for h in hs]
        b_row = [jnp.sum(jnp.where(mask_t, f_col[h], 0.0), axis=0, keepdims=True) for h in hs]
        total = [jnp.sum(f_row[h], axis=1, keepdims=True) for h in hs]
        m_prev = [m_s[h] for h in hs]
        q = [q_ref[0, pl.ds(r0, chunk), h * dqk:(h + 1) * dqk] * scale for h in hs]
        k = [k_ref[0, pl.ds(r0, chunk), h * dqk:(h + 1) * dqk] for h in hs]
        vb = [v_ref[0, pl.ds(r0, chunk), h * dv:(h + 1) * dv].astype(BF16) for h in hs]
        c_prev = [c_s[h] for h in hs]
        n_prev = [n_s[h] for h in hs]
        qkc = [_mm(q[h].astype(BF16),
                   jnp.concatenate([k[h].T.astype(BF16), c_prev[h].astype(BF16)], axis=1))
               for h in hs]
        dmat = [jnp.where(mask, b_col[h] + (i_row[h] - b_row[h]), -jnp.inf) for h in hs]
        inter = [b_col[h] + m_prev[h] for h in hs]
        m_row = [jnp.maximum(inter[h], jnp.max(dmat[h], axis=1, keepdims=True)) for h in hs]
        w_inter = [jnp.exp(inter[h] - m_row[h]) for h in hs]
        sqk = [qkc[h][:, :chunk] * jnp.exp(dmat[h] - m_row[h]) for h in hs]
        dlast = [total[h] + (i_col[h] - b_col[h]) for h in hs]
        m_new = [jnp.maximum(total[h] + m_prev[h], jnp.max(dlast[h], axis=0, keepdims=True))
                 for h in hs]
        kw = [k[h] * jnp.exp(dlast[h] - m_new[h]) for h in hs]
        both = [_mm(jnp.concatenate([sqk[h].astype(BF16), kw[h].T.astype(BF16)], axis=0), vb[h])
                for h in hs]
        for h in hs:
            num = w_inter[h] * qkc[h][:, chunk:] + both[h][:chunk]
            den = (w_inter[h] * jnp.sum(q[h] * n_prev[h], axis=1, keepdims=True)
                   + jnp.sum(sqk[h], axis=1, keepdims=True))
            h_ref[0, pl.ds(r0, chunk), h * dv:(h + 1) * dv] = (
                num / jnp.maximum(jnp.abs(den), jnp.exp(-m_row[h])))
            decay = jnp.exp(total[h] + m_prev[h] - m_new[h])
            c_s[h] = decay * c_prev[h] + both[h][chunk:]
            n_s[h] = decay * n_prev[h] + jnp.sum(kw[h], axis=0, keepdims=True)
            m_s[h] = m_new[h]
        return carry

    lax.fori_loop(0, ncb, step, 0)


def _mlstm_scan(proj, gcol, grow, bcol, brow, *, reverse):
    b, s, _ = proj.shape
    heads = A_HEADS
    dqk = proj.shape[2] // (6 * heads)
    dv = 2 * dqk
    nq, nv = heads * dqk, heads * dv
    blk = min(SCAN_BLOCK, s)
    nb = s // blk
    chunk = gcol.shape[2]
    ncb = blk // chunk
    ng = gcol.shape[3]

    def seq(j):
        return (nb - 1 - j) if reverse else j

    return pl.pallas_call(
        functools.partial(_mlstm_kernel, reverse=reverse, chunk=chunk, heads=heads,
                          dqk=dqk, dv=dv),
        out_shape=jax.ShapeDtypeStruct((b, s, nv), F32),
        grid=(b, nb),
        in_specs=[pl.BlockSpec((1, blk, nq), lambda i, j: (i, seq(j), 0)),
                  pl.BlockSpec((1, blk, nq), lambda i, j: (i, seq(j), 1)),
                  pl.BlockSpec((1, blk, nv), lambda i, j: (i, seq(j), 1)),
                  pl.BlockSpec((1, ncb, chunk, ng), lambda i, j: (i, seq(j), 0, 0)),
                  pl.BlockSpec((1, ncb, ng, chunk), lambda i, j: (i, seq(j), 0, 0)),
                  pl.BlockSpec((1, ng), lambda i, j: (0, 0)),
                  pl.BlockSpec((ng, 1), lambda i, j: (0, 0))],
        out_specs=pl.BlockSpec((1, blk, nv), lambda i, j: (i, seq(j), 0)),
        scratch_shapes=[pltpu.VMEM((heads, dqk, dv), F32),
                        pltpu.VMEM((heads, 1, dqk), F32),
                        pltpu.VMEM((heads, 1, 1), F32)],
        compiler_params=_params("parallel", "arbitrary"),
        name="mlstm_scan_bwd" if reverse else "mlstm_scan_fwd",
    )(proj, proj, proj, gcol, grow, bcol, brow)


def _seq_halo_specs(tm, s, width, col_block=0):
    r = tm // HALO
    last = s // HALO - 1

    def prev(i, j):
        return (i, jnp.maximum(j * r - 1, 0), col_block)

    def nxt(i, j):
        return (i, jnp.minimum((j + 1) * r, last), col_block)

    return [pl.BlockSpec((1, HALO, width), prev),
            pl.BlockSpec((1, tm, width), lambda i, j: (i, j, col_block)),
            pl.BlockSpec((1, HALO, width), nxt)]


def _halo_valid(tm):
    j = pl.program_id(1)
    rid = lax.broadcasted_iota(jnp.int32, (tm + 2 * HALO, 1), 0)
    return jnp.logical_and(jnp.logical_or(rid >= HALO, j > 0),
                           jnp.logical_or(rid < tm + HALO, j < pl.num_programs(1) - 1))


def _dwconv_rows(ext, w_ref, c0, c1, taps, tm):
    n = ext.shape[0]
    acc = None
    for t in range(taps):
        off = t - taps // 2
        src = ext if off == 0 else pltpu.roll(ext, (-off) % n, 0)
        term = src * w_ref[t:t + 1, c0:c1]
        acc = term if acc is None else acc + term
    return acc[HALO:HALO + tm]


def _gdn_prep(proj, conv_w):
    b, s, n = proj.shape
    heads = B_HEADS
    dk = n // (4 * heads)
    width = 3 * heads * dk
    tm = min(ROW_TILE, s)
    specs = []
    for cb in range(3):
        specs += _seq_halo_specs(tm, s, heads * dk, cb)
    return pl.pallas_call(
        functools.partial(_gdn_prep_kernel, tm=tm, heads=heads, dk=dk),
        out_shape=jax.ShapeDtypeStruct((b, s, width), F32),
        grid=(b, s // tm),
        in_specs=specs + [pl.BlockSpec((B_CONV, width), lambda i, j: (0, 0))],
        out_specs=pl.BlockSpec((1, tm, width), lambda i, j: (i, j, 0)),
        compiler_params=_params("parallel", "parallel"),
        name="gdn_prep",
    )(*([proj] * 9), conv_w)


def _gdn_prep_kernel(qp, q, qn, kp, k, kn, vp, v, vn, w_ref, o_ref, *, tm, heads, dk):
    valid = _halo_valid(tm)
    hw = heads * dk
    for part, (p_ref, x_ref, n_ref) in enumerate(((qp, q, qn), (kp, k, kn), (vp, v, vn))):
        for h in range(heads):
            c0, c1 = h * dk, (h + 1) * dk
            ext = jnp.concatenate([p_ref[0, :, c0:c1], x_ref[0, :, c0:c1], n_ref[0, :, c0:c1]], axis=0)
            ext = jnp.where(valid, ext, 0.0)
            y = _dwconv_rows(ext, w_ref, part * hw + c0, part * hw + c1, B_CONV, tm)
            y = y * _sigmoid(y)
            if part < 2:
                y = y * lax.rsqrt(jnp.sum(y * y, axis=-1, keepdims=True) + EPS)
            if part == 0:
                y = y * (dk ** -0.5)
            o_ref[0, :, part * hw + c0:part * hw + c1] = y


def _gdn_kernel(q_ref, k_ref, v_ref, gcol_ref, grow_ref, pcol_ref, prow_ref, o_ref,
                s_s, mq_s, r_s, gl_s, *, reverse, chunk, group, heads, dk):
    @pl.when(pl.program_id(1) == 0)
    def _():
        s_s[...] = jnp.zeros_like(s_s)

    blk = q_ref.shape[1]
    cpg = group // chunk
    clog = chunk.bit_length() - 1
    row = lax.broadcasted_iota(jnp.int32, (group, group), 0)
    col = lax.broadcasted_iota(jnp.int32, (group, group), 1)

    def same_block(log2):
        return jnp.right_shift(row, log2) == jnp.right_shift(col, log2)

    same_chunk = same_block(clog)
    causal = jnp.logical_and(same_chunk, (col >= row) if reverse else (col <= row))
    causal_t = jnp.logical_and(same_chunk, (row >= col) if reverse else (row <= col))
    strict = jnp.logical_and(same_chunk, (col > row) if reverse else (col < row))
    eye = (row == col).astype(F32)
    base_log2 = 3
    same_base = same_block(base_log2)
    level_masks = [jnp.logical_and(same_block(lg + 1), jnp.logical_not(same_block(lg)))
                   for lg in range(base_log2, clog)]
    chunk_of_row = jnp.right_shift(lax.broadcasted_iota(jnp.int32, (group, 1), 0), clog)

    raw_col = gcol_ref[0]
    raw_row = grow_ref[0]
    g_cols = -jnp.exp(pcol_ref[0:1, :]) * _softplus(raw_col[:, :heads] + pcol_ref[1:2, :])
    g_rows = -jnp.exp(prow_ref[:, 0:1]) * _softplus(raw_row[:heads, :] + prow_ref[:, 1:2])
    beta_cols = _sigmoid(raw_col[:, heads:])

    def par(fn, *lists):
        return [fn(*args) for args in zip(*lists)]

    def chunk_local(units):
        rows = [slice(gi * group, (gi + 1) * group) for _, gi in units]
        cols = [slice(h * dk, (h + 1) * dk) for h, _ in units]
        g_col = [g_cols[r, h:h + 1] for (h, _), r in zip(units, rows)]
        g_row = [g_rows[h:h + 1, r] for (h, _), r in zip(units, rows)]
        beta = [beta_cols[r, h:h + 1] for (h, _), r in zip(units, rows)]
        gc_col = par(lambda g: jnp.sum(jnp.where(causal, g, 0.0), axis=1, keepdims=True), g_row)
        gc_row = par(lambda g: jnp.sum(jnp.where(causal_t, g, 0.0), axis=0, keepdims=True), g_col)
        tot_col = par(lambda g: jnp.sum(jnp.where(same_chunk, g, 0.0), axis=1, keepdims=True), g_row)
        decay = par(lambda c, r: jnp.exp(jnp.where(causal, c - r, -jnp.inf)), gc_col, gc_row)
        q = [q_ref[0, r, c] for r, c in zip(rows, cols)]
        k = [k_ref[0, r, c] for r, c in zip(rows, cols)]
        v = [v_ref[0, r, c] for r, c in zip(rows, cols)]
        kb = par(lambda x: x.astype(BF16), k)
        kq = par(lambda kb_, q_: _mm_nt(jnp.concatenate([kb_, q_.astype(BF16)], axis=0), kb_),
                 kb, q)
        a = par(lambda kq_, d, b_: jnp.where(strict, kq_[:group] * d, 0.0) * b_, kq, decay, beta)
        n1 = par(lambda a_: jnp.where(same_base, a_, 0.0), a)
        n2 = par(lambda n: _mm_inv(n, n), n1)
        n4 = par(lambda n: _mm_inv(n, n), n2)
        x = par(lambda n, m: _mm_inv(eye - n, eye + m), n1, n2)
        x = par(lambda x_, n: _mm_inv(x_, eye + n), x, n4)
        for lm in level_masks:
            xl = par(lambda x_, a_: _mm_inv(x_, jnp.where(lm, a_, 0.0)), x, a)
            x = par(lambda x_, xl_: x_ - _mm_inv(xl_, x_), x, xl)
        eg = par(jnp.exp, gc_col)
        wu = par(lambda x_, k_, b_, e, v_: _mm_inv(
            x_, jnp.concatenate([k_ * b_ * e, v_ * b_], axis=1)).astype(BF16), x, k, beta, eg, v)
        qo = par(lambda kq_, d, wu_: _mm((kq_[group:] * d).astype(BF16), wu_), kq, decay, wu)
        kd_by_chunk = par(lambda k_, t_, c_: jnp.concatenate(
            [jnp.where(chunk_of_row == c, k_ * jnp.exp(t_ - c_), 0.0) for c in range(cpg)],
            axis=1).astype(BF16), k, tot_col, gc_col)
        mr = par(_mm_tn, kd_by_chunk, wu)
        for i, (h, gi) in enumerate(units):
            o_ref[0, rows[i], cols[i]] = qo[i][:, dk:]
            q_prime = (q[i] * eg[i] - qo[i][:, :dk]).astype(BF16)
            for c in range(cpg):
                cg = gi * cpg + c
                mq_s[h, cg, 0:dk, :] = mr[i][c * dk:(c + 1) * dk, :dk].astype(BF16)
                mq_s[h, cg, dk:dk + chunk, :] = q_prime[c * chunk:(c + 1) * chunk]
                r_s[h, cg] = mr[i][c * dk:(c + 1) * dk, dk:]
                total = jnp.sum(g_row[i][:, c * chunk:(c + 1) * chunk], axis=1, keepdims=True)
                gl_s[h, cg] = jnp.broadcast_to(jnp.exp(total), (1, dk))

    all_units = [(h, gi) for h in range(heads) for gi in range(blk // group)]
    for u0 in range(0, len(all_units), GDN_UNITS):
        chunk_local(all_units[u0:u0 + GDN_UNITS])

    ncb = blk // chunk
    for ci in range(ncb):
        cc = (ncb - 1 - ci) if reverse else ci
        s_prev = [s_s[h] for h in range(heads)]
        p = [_mm(mq_s[h, cc], s_prev[h].astype(BF16)) for h in range(heads)]
        for h in range(heads):
            s_s[h] = gl_s[h, cc] * s_prev[h] - p[h][:dk] + r_s[h, cc]
            o_ref[0, cc * chunk:(cc + 1) * chunk, h * dk:(h + 1) * dk] += p[h][dk:]


def _gdn_scan(qkv, gcol, grow, pcol, prow, *, reverse):
    b, s, n = qkv.shape
    heads = B_HEADS
    hw = n // 3
    dk = hw // heads
    blk = min(SCAN_BLOCK, s)
    nb = s // blk
    ncb = blk // SCAN_CHUNK
    ng = gcol.shape[2]

    def seq(j):
        return (nb - 1 - j) if reverse else j

    return pl.pallas_call(
        functools.partial(_gdn_kernel, reverse=reverse, chunk=SCAN_CHUNK, group=GDN_GROUP,
                          heads=heads, dk=dk),
        out_shape=jax.ShapeDtypeStruct((b, s, hw), F32),
        grid=(b, nb),
        in_specs=[pl.BlockSpec((1, blk, hw), lambda i, j: (i, seq(j), 0)),
                  pl.BlockSpec((1, blk, hw), lambda i, j: (i, seq(j), 1)),
                  pl.BlockSpec((1, blk, hw), lambda i, j: (i, seq(j), 2)),
                  pl.BlockSpec((1, blk, ng), lambda i, j: (i, seq(j), 0)),
                  pl.BlockSpec((1, ng, blk), lambda i, j: (i, 0, seq(j))),
                  pl.BlockSpec((2, heads), lambda i, j: (0, 0)),
                  pl.BlockSpec((heads, 2), lambda i, j: (0, 0))],
        out_specs=pl.BlockSpec((1, blk, hw), lambda i, j: (i, seq(j), 0)),
        scratch_shapes=[pltpu.VMEM((heads, dk, dk), F32),
                        pltpu.VMEM((heads, ncb, dk + SCAN_CHUNK, dk), BF16),
                        pltpu.VMEM((heads, ncb, dk, dk), F32),
                        pltpu.VMEM((heads, ncb, 1, dk), F32)],
        compiler_params=_params("parallel", "arbitrary"),
        name="gdn_scan_bwd" if reverse else "gdn_scan_fwd",
    )(qkv, qkv, qkv, gcol, grow, pcol, prow)


def _rel_bucket(rel):
    nb = REL_BUCKETS // 2
    exact = nb // 2
    n = jnp.abs(rel)
    large = exact + (jnp.log(jnp.maximum(n, 1).astype(jnp.float32) / exact)
                     / math.log(REL_MAX_DIST / exact) * (nb - exact)).astype(jnp.int32)
    large = jnp.minimum(large, nb - 1)
    return jnp.where(rel > 0, nb, 0) + jnp.where(n < exact, n, large)


def _bias_tiles(rel_bias, t):
    assert t >= REL_MAX_DIST
    n = 2 * t + 1
    m = jnp.arange(n, dtype=jnp.int32)
    rel_in_tile = jnp.where(m < t, m, m - n)
    d = jnp.arange(-2, 3, dtype=jnp.int32)
    rel = d[:, None] * t + rel_in_tile[None, :]
    vals = jnp.transpose(rel_bias.astype(F32)[_rel_bucket(rel)], (2, 0, 1)) * LOG2E
    h = vals.shape[0]
    tiled = jnp.tile(vals, (1, 1, t))[:, :, :t * (n - 1)].reshape(h, 5, t, n - 1)
    return tiled[:, :, :, :t]


def _attn_kernel(q_ref, k_ref, v_ref, bias_ref, lam_ref, ng_ref, o_ref,
                 m1_s, a1_s, m2_s, a2_s, vext_s, *, lambda_init, dh):
    qi = pl.program_id(2)
    ki = pl.program_id(3)
    t = bias_ref.shape[2]
    nq = q_ref.shape[1] // t
    tk, dv = v_ref.shape[1], v_ref.shape[2]
    r = tk // t
    maps = ((m1_s, a1_s), (m2_s, a2_s))

    @pl.when(ki == 0)
    def _():
        for m_s, a_s in maps:
            m_s[...] = jnp.full_like(m_s, -jnp.inf)
            a_s[...] = jnp.zeros_like(a_s)
        vext_s[:, dv:] = jnp.ones((tk, dv), BF16)

    vext_s[:, :dv] = v_ref[0]
    k = k_ref[0]
    first = ki * r - qi * nq

    def update(with_bias):
        for iq in range(nq):
            rows = slice(iq * t, (iq + 1) * t)
            off = first - iq
            if with_bias:
                bias = jnp.concatenate(
                    [bias_ref[0, jnp.clip(off + j, -2, 2) + 2] for j in range(r)], axis=1)
                shift = 0.0
            else:
                shift = bias_ref[0, jnp.clip(off, -2, 2) + 2, 0:1, 0:1]
            s = [_mm_nt(q_ref[0, rows, i * dh:(i + 1) * dh], k[:, i * dh:(i + 1) * dh])
                 for i in range(2)]
            if with_bias:
                s = [s_ + bias for s_ in s]
            m_prev = [m_s[rows] for m_s, _ in maps]
            m_new = [jnp.maximum(m_prev[i], jnp.max(s[i], axis=1, keepdims=True) + shift)
                     for i in range(2)]
            p = [jnp.exp2(s[i] - (m_new[i] - shift)).astype(BF16) for i in range(2)]
            pv = [_mm(p[i], vext_s[...]) for i in range(2)]
            for i, (m_s, a_s) in enumerate(maps):
                a_s[rows] = jnp.exp2(m_prev[i] - m_new[i]) * a_s[rows] + pv[i]
                m_s[rows] = m_new[i]

    far = jnp.logical_or(first - (nq - 1) >= 2, first + (r - 1) <= -2)

    @pl.when(far)
    def _():
        update(False)

    @pl.when(jnp.logical_not(far))
    def _():
        update(True)

    @pl.when(ki == pl.num_programs(3) - 1)
    def _():
        lp = lam_ref[...]
        lam = (jnp.exp(jnp.sum(lp[0:1] * lp[1:2], axis=1, keepdims=True))
               - jnp.exp(jnp.sum(lp[2:3] * lp[3:4], axis=1, keepdims=True)) + lambda_init)
        a1, a2 = a1_s[...], a2_s[...]
        o = a1[:, :dv] / a1[:, dv:dv + 1] - lam * (a2[:, :dv] / a2[:, dv:dv + 1])
        o_ref[0] = _rms_rows(o) * ng_ref[...] * (1.0 - lambda_init)


def _diff_attn(proj, bias, lam_p, norm_g, lambda_init):
    b, s, n = proj.shape
    heads = C_HEADS
    dv = n // (3 * heads)
    t = bias.shape[2]
    tk = min(ATTN_KEYS, s)
    tq = min(ATTN_ROWS, s)
    return pl.pallas_call(
        functools.partial(_attn_kernel, lambda_init=lambda_init, dh=dv // 2),
        out_shape=jax.ShapeDtypeStruct((b, s, heads * dv), F32),
        grid=(b, heads, s // tq, s // tk),
        in_specs=[pl.BlockSpec((1, tq, dv), lambda i, h, qi, ki: (i, qi, h)),
                  pl.BlockSpec((1, tk, dv), lambda i, h, qi, ki: (i, ki, heads + h)),
                  pl.BlockSpec((1, tk, dv), lambda i, h, qi, ki: (i, ki, 2 * heads + h)),
                  pl.BlockSpec((1, 5, t, t), lambda i, h, qi, ki: (h, 0, 0, 0)),
                  pl.BlockSpec(lam_p.shape, lambda i, h, qi, ki: (0, 0)),
                  pl.BlockSpec((1, dv), lambda i, h, qi, ki: (0, 0))],
        out_specs=pl.BlockSpec((1, tq, dv), lambda i, h, qi, ki: (i, qi, h)),
        scratch_shapes=[pltpu.VMEM((tq, 1), F32), pltpu.VMEM((tq, 2 * dv), F32),
                        pltpu.VMEM((tq, 1), F32), pltpu.VMEM((tq, 2 * dv), F32),
                        pltpu.VMEM((tk, 2 * dv), BF16)],
        compiler_params=_params("parallel", "parallel", "parallel", "arbitrary"),
        name="diff_attn",
    )(proj, proj, proj, bias, lam_p, norm_g.reshape(1, dv))


def _outproj_kernel(*refs, mode, hd):
    if mode == "attn":
        a_ref, w_ref, x_ref, g_ref, o_ref = refs
        act = a_ref[0]
    else:
        a_ref, b_ref, gate_ref, ng_ref, w_ref, x_ref, g_ref, o_ref = refs
        ssum = a_ref[0] + b_ref[0]
        d = ssum.shape[1]
        hs = jnp.concatenate([_rms_rows(ssum[:, c0:c0 + hd]) for c0 in range(0, d, hd)], axis=1)
        hs = hs * ng_ref[...]
        gate = gate_ref[0]
        if mode == "mlstm":
            act = _sigmoid(gate) * hs
        else:
            act = hs * (gate * _sigmoid(gate))
    o_ref[0] = x_ref[0] + g_ref[0] * _mm(act.astype(BF16), w_ref[...])


def _outproj(mode, x, mod, w, a, b=None, gate_src=None, gate_block=0, norm_g=None, hd=0):
    bsz, s, d = x.shape
    tm = min(ROW_TILE, s)
    row = pl.BlockSpec((1, tm, d), lambda i, j: (i, j, 0))
    in_specs, args = [row], [a]
    if mode != "attn":
        in_specs += [row, pl.BlockSpec((1, tm, d), lambda i, j: (i, j, gate_block)),
                     pl.BlockSpec((1, d), lambda i, j: (0, 0))]
        args += [b, gate_src, norm_g.reshape(1, d)]
    in_specs += [pl.BlockSpec((d, d), lambda i, j: (0, 0)), row,
                 pl.BlockSpec((1, 1, d), lambda i, j: (i, 0, 2))]
    args += [w, x, mod]
    return pl.pallas_call(
        functools.partial(_outproj_kernel, mode=mode, hd=hd),
        out_shape=jax.ShapeDtypeStruct((bsz, s, d), F32),
        grid=(bsz, s // tm),
        in_specs=in_specs,
        out_specs=row,
        compiler_params=_params("parallel", "parallel"),
        name="outproj_" + mode,
    )(*args)


def _ffn_kernel(*refs, tm, dff, tn, final):
    if final:
        xp_ref, x_ref, xn_ref, sh_ref, sc_ref, g_ref, wu_ref, cw_ref, cb_ref, wd_ref, fg_ref, o_ref = refs
    else:
        xp_ref, x_ref, xn_ref, sh_ref, sc_ref, g_ref, wu_ref, cw_ref, cb_ref, wd_ref, o_ref = refs
    x = x_ref[0]
    ext = jnp.concatenate([xp_ref[0], x, xn_ref[0]], axis=0)
    hn = _rms_rows(ext) * (1.0 + sc_ref[0]) + sh_ref[0]
    hn = jnp.where(_halo_valid(tm), hn, 0.0).astype(BF16)
    acc = jnp.zeros_like(x)
    for c0 in range(0, dff, tn):
        ua = _dwconv_rows(_mm(hn, wu_ref[:, c0:c0 + tn]), cw_ref, c0, c0 + tn, FFN_CONV, tm)
        ug = _dwconv_rows(_mm(hn, wu_ref[:, dff + c0:dff + c0 + tn]), cw_ref, dff + c0, dff + c0 + tn,
                          FFN_CONV, tm)
        ua = ua + cb_ref[:, c0:c0 + tn]
        ug = ug + cb_ref[:, dff + c0:dff + c0 + tn]
        act = ua * (ug * _sigmoid(ug))
        acc = acc + _mm(act.astype(BF16), wd_ref[c0:c0 + tn, :])
    y = x + g_ref[0] * acc
    if final:
        y = _rms_rows(y) * fg_ref[...]
    o_ref[0] = y


def _conv_ffn(x, mod, w_up, conv_w, conv_b, w_down, final_g=None):
    b, s, d = x.shape
    dff = w_down.shape[0]
    tm = min(ROW_TILE, s)
    final = final_g is not None
    const = lambda i, j: (0, 0)
    in_specs = _seq_halo_specs(tm, s, d) + [
        pl.BlockSpec((1, 1, d), lambda i, j: (i, 0, 3)),
        pl.BlockSpec((1, 1, d), lambda i, j: (i, 0, 4)),
        pl.BlockSpec((1, 1, d), lambda i, j: (i, 0, 5)),
        pl.BlockSpec(w_up.shape, const),
        pl.BlockSpec(conv_w.shape, const),
        pl.BlockSpec((1, 2 * dff), const),
        pl.BlockSpec(w_down.shape, const)]
    args = [x, x, x, mod, mod, mod, w_up, conv_w, conv_b.reshape(1, 2 * dff), w_down]
    if final:
        in_specs.append(pl.BlockSpec((1, d), const))
        args.append(final_g.reshape(1, d))
    return pl.pallas_call(
        functools.partial(_ffn_kernel, tm=tm, dff=dff, tn=FFN_COLS, final=final),
        out_shape=jax.ShapeDtypeStruct((b, s, d), F32),
        grid=(b, s // tm),
        in_specs=in_specs,
        out_specs=pl.BlockSpec((1, tm, d), lambda i, j: (i, j, 0)),
        compiler_params=_params("parallel", "parallel"),
        name="conv_ffn",
    )(*args)


def _mlstm_layer(x, mod, w_in, gate_b, norm_g, w_out):
    heads = A_HEADS
    n_main = w_in.shape[1] - 4 * heads
    proj, gates = _inproj(x, mod, _split_w_in(w_in, n_main), n_main)
    hs = []
    for direction in range(2):
        lo = 2 * heads * direction
        gcol, grow = _gate_layouts(gates, lo, 2 * heads, min(MLSTM_CHUNK, x.shape[1]))
        bias = gate_b[lo:lo + 2 * heads].astype(F32)
        hs.append(_mlstm_scan(proj, gcol, grow, bias.reshape(1, -1), bias.reshape(-1, 1),
                              reverse=direction == 1))
    d = x.shape[2]
    return _outproj("mlstm", x, mod, w_out.astype(BF16), hs[0], hs[1], gate_src=proj,
                    gate_block=n_main // d - 1, norm_g=norm_g, hd=d // heads)


def _gdn_layer(x, mod, w_in, conv_w, a_log, dt_bias, norm_g, w_out):
    heads = B_HEADS
    n_main = w_in.shape[1] - 4 * heads
    proj, gates = _inproj(x, mod, _split_w_in(w_in, n_main), n_main)
    qkv = _gdn_prep(proj, conv_w)
    outs = []
    for direction in range(2):
        lo = 2 * heads * direction
        gcol = gates[:, :, lo:lo + 2 * heads]
        grow = jnp.swapaxes(gcol, 1, 2)
        p = jnp.stack([a_log[direction], dt_bias[direction]]).astype(F32)
        outs.append(_gdn_scan(qkv, gcol, grow, p, p.T, reverse=direction == 1))
    d = x.shape[2]
    return _outproj("gdn", x, mod, w_out.astype(BF16), outs[0], outs[1], gate_src=proj,
                    gate_block=n_main // d - 1, norm_g=jnp.tile(norm_g, heads), hd=d // heads)


def _attn_layer(x, mod, w_in, lam_p, norm_g, w_out, bias, lambda_init):
    n = w_in.shape[1]
    dh = n // (6 * C_HEADS)
    proj = _inproj(x, mod, w_in.astype(BF16), n, out_dtype=BF16, lead_cols=n // 3,
                   lead_scale=dh ** -0.5 * LOG2E)[0]
    o = _diff_attn(proj, bias, lam_p.astype(F32), norm_g, lambda_init)
    return _outproj("attn", x, mod, w_out.astype(BF16), o)


def _trunk(x, mods, bias_tiles, ada_w, ada_b, a_w_in, a_gate_b, a_norm_g, a_w_out, b_w_in, b_conv_w,
           b_a_log, b_dt_bias, b_norm_g, b_w_out, c_w_in, c_lambda, c_norm_g, c_w_out, rel_bias,
           ffn_w_up, ffn_conv_w, ffn_conv_b, ffn_w_down, final_g):
    depth = ada_w.shape[0]
    for i in range(depth):
        mod = mods[i][:, None, :]
        j = i // N_MIXERS
        kind = i % N_MIXERS
        if kind == 0:
            x = _mlstm_layer(x, mod, a_w_in[j], a_gate_b[j], a_norm_g[j], a_w_out[j])
        elif kind == 1:
            x = _gdn_layer(x, mod, b_w_in[j], b_conv_w[j], b_a_log[j], b_dt_bias[j], b_norm_g[j],
                           b_w_out[j])
        else:
            x = _attn_layer(x, mod, c_w_in[j], c_lambda[j], c_norm_g[j], c_w_out[j], bias_tiles,
                            0.8 - 0.6 * math.exp(-0.3 * i))
        x = _conv_ffn(x, mod, ffn_w_up[i].astype(BF16), ffn_conv_w[i], ffn_conv_b[i],
                      ffn_w_down[i].astype(BF16), final_g if i == depth - 1 else None)
    return x


def kernel(x_prompt, x_sample, c_prompt, c_sample, ada_w, ada_b, a_w_in, a_gate_b, a_norm_g, a_w_out, b_w_in, b_conv_w, b_a_log, b_dt_bias, b_norm_g, b_w_out, c_w_in, c_lambda, c_norm_g, c_w_out, rel_bias, ffn_w_up, ffn_conv_w, ffn_conv_b, ffn_w_down, final_g):
    weights = (ada_w, ada_b, a_w_in, a_gate_b, a_norm_g, a_w_out, b_w_in, b_conv_w, b_a_log,
               b_dt_bias, b_norm_g, b_w_out, c_w_in, c_lambda, c_norm_g, c_w_out, rel_bias,
               ffn_w_up, ffn_conv_w, ffn_conv_b, ffn_w_down, final_g)
    nb_p, nb_s = c_prompt.shape[0], c_sample.shape[0]
    c_all = jnp.concatenate([c_prompt, c_sample], axis=0)
    pad = (-c_all.shape[0]) % SUBLANES
    mods = _ada_mod(jnp.pad(c_all, ((0, pad), (0, 0))), ada_w, ada_b)
    outs, bias_tiles = [], {}
    for x, lo, n in ((x_prompt, 0, nb_p), (x_sample, nb_p, nb_s)):
        t = min(ATTN_TILE, x.shape[1])
        if t not in bias_tiles:
            bias_tiles[t] = _bias_tiles(rel_bias, t)
        outs.append(_trunk(x, mods[:, lo:lo + n], bias_tiles[t], *weights))
    return tuple(outs)
```

```python
import functools
import math

import jax
import jax.numpy as jnp
from jax import lax
from jax.experimental import pallas as pl
from jax.experimental.pallas import tpu as pltpu

F32 = jnp.float32
BF16 = jnp.bfloat16
EPS = 1e-6
LOG2E = math.log2(math.e)

N_MIXERS = 3
A_HEADS = 4
B_HEADS = 8
B_CONV = 5
C_HEADS = 8
REL_BUCKETS = 32
REL_MAX_DIST = 128
FFN_CONV = 3

LANES = 128
SUBLANES = 8
HALO = SUBLANES
VMEM_LIMIT = 56 * 1024 * 1024

ROW_TILE = 512
SCAN_BLOCK = 512
SCAN_CHUNK = 64
GDN_GROUP = 256
GDN_UNITS = 4
MLSTM_CHUNK = 256
ATTN_TILE = 512
ATTN_ROWS = 2048
ATTN_KEYS = 2048
FFN_COLS = 1408


def _params(*sem):
    return pltpu.CompilerParams(dimension_semantics=sem, vmem_limit_bytes=VMEM_LIMIT)


def _sigmoid(x):
    return 1.0 / (1.0 + jnp.exp(-x))


def _softplus(x):
    return jnp.maximum(x, 0.0) + jnp.log(1.0 + jnp.exp(-jnp.abs(x)))


def _rms_rows(x):
    return x * lax.rsqrt(jnp.mean(x * x, axis=-1, keepdims=True) + EPS)


def _mm(a, b):
    return jnp.dot(a, b, preferred_element_type=F32)


def _mm_inv(a, b):
    return _mm(a.astype(BF16), b.astype(BF16))


def _mm_nt(a, b):
    return lax.dot_general(a, b, (((1,), (1,)), ((), ())), preferred_element_type=F32)


def _mm_tn(a, b):
    return lax.dot_general(a, b, (((0,), (0,)), ((), ())), preferred_element_type=F32)


def _mod_kernel(c_ref, w_ref, b_ref, o_ref):
    c = c_ref[...]
    cs = c * _sigmoid(c)
    o_ref[0] = _mm(cs.astype(BF16), w_ref[0].astype(BF16)) + b_ref[0]


def _ada_mod(c, ada_w, ada_b):
    depth, d, n = ada_w.shape
    bp = c.shape[0]
    tn = 1536
    return pl.pallas_call(
        _mod_kernel,
        out_shape=jax.ShapeDtypeStruct((depth, bp, n), F32),
        grid=(depth, n // tn),
        in_specs=[pl.BlockSpec((bp, d), lambda l, j: (0, 0)),
                  pl.BlockSpec((1, d, tn), lambda l, j: (l, 0, j)),
                  pl.BlockSpec((1, 1, tn), lambda l, j: (l, 0, j))],
        out_specs=pl.BlockSpec((1, bp, tn), lambda l, j: (l, 0, j)),
        compiler_params=_params("parallel", "parallel"),
        name="ada_mod",
    )(c, ada_w, ada_b.reshape(depth, 1, n))


def _inproj_kernel(x_ref, sh_ref, sc_ref, w_ref, *o_refs, n_main, tn, lead_cols, lead_scale):
    hn = (_rms_rows(x_ref[0]) * (1.0 + sc_ref[0]) + sh_ref[0]).astype(BF16)
    for c0 in range(0, n_main, tn):
        y = _mm(hn, w_ref[:, c0:c0 + tn])
        if c0 < lead_cols:
            y = y * lead_scale
        o_refs[0][0, :, c0:c0 + tn] = y.astype(o_refs[0].dtype)
    if len(o_refs) > 1:
        o_refs[1][0] = _mm(hn, w_ref[:, n_main:])


def _inproj(x, mod, w, n_main, out_dtype=F32, lead_cols=0, lead_scale=1.0):
    b, s, d = x.shape
    n_all = w.shape[1]
    tm = min(ROW_TILE, s)
    tn = 512
    assert lead_cols % tn == 0
    out_shape = [jax.ShapeDtypeStruct((b, s, n_main), out_dtype)]
    out_specs = [pl.BlockSpec((1, tm, n_main), lambda i, j: (i, j, 0))]
    if n_all > n_main:
        out_shape.append(jax.ShapeDtypeStruct((b, s, n_all - n_main), F32))
        out_specs.append(pl.BlockSpec((1, tm, n_all - n_main), lambda i, j: (i, j, 0)))
    return pl.pallas_call(
        functools.partial(_inproj_kernel, n_main=n_main, tn=tn, lead_cols=lead_cols,
                          lead_scale=lead_scale),
        out_shape=out_shape,
        grid=(b, s // tm),
        in_specs=[pl.BlockSpec((1, tm, d), lambda i, j: (i, j, 0)),
                  pl.BlockSpec((1, 1, d), lambda i, j: (i, 0, 0)),
                  pl.BlockSpec((1, 1, d), lambda i, j: (i, 0, 1)),
                  pl.BlockSpec((d, n_all), lambda i, j: (0, 0))],
        out_specs=out_specs,
        compiler_params=_params("parallel", "parallel"),
        name="inproj",
    )(x, mod, mod, w)


def _split_w_in(w, n_main):
    d, n = w.shape
    if n == n_main:
        return w.astype(BF16)
    pad = LANES - (n - n_main)
    return jnp.pad(w, ((0, 0), (0, pad))).astype(BF16)


def _gate_layouts(g, lo, width, chunk):
    b, s, _ = g.shape
    col = g[:, :, lo:lo + width].reshape(b, s // chunk, chunk, width)
    return col, jnp.swapaxes(col, 2, 3)


def _mlstm_kernel(q_ref, k_ref, v_ref, gcol_ref, grow_ref, bcol_ref, brow_ref, h_ref,
                  c_s, n_s, m_s, *, reverse, chunk, heads, dqk, dv):
    @pl.when(pl.program_id(1) == 0)
    def _():
        c_s[...] = jnp.zeros_like(c_s)
        n_s[...] = jnp.zeros_like(n_s)
        m_s[...] = jnp.zeros_like(m_s)

    ncb = gcol_ref.shape[1]
    row = lax.broadcasted_iota(jnp.int32, (chunk, chunk), 0)
    col = lax.broadcasted_iota(jnp.int32, (chunk, chunk), 1)
    mask = (col >= row) if reverse else (col <= row)
    mask_t = (row >= col) if reverse else (row <= col)
    scale = dqk ** -0.5

    def step(ci, carry):
        cc = (ncb - 1 - ci) if reverse else ci
        r0 = pl.multiple_of(cc * chunk, chunk)
        gcol = gcol_ref[0, cc] + bcol_ref[...]
        grow = grow_ref[0, cc] + brow_ref[...]
        fcol = -_softplus(-gcol)
        frow = -_softplus(-grow)
        hs = range(heads)
        i_col = [gcol[:, h:h + 1] for h in hs]
        f_col = [fcol[:, heads + h:heads + h + 1] for h in hs]
        i_row = [grow[h:h + 1, :] for h in hs]
        f_row = [frow[heads + h:heads + h + 1, :] for h in hs]
        b_col = [jnp.sum(jnp.where(mask, f_row[h], 0.0), axis=1, keepdims=True) for h in hs]
        b_row = [jnp.sum(jnp.where(mask_t, f_col[h], 0.0), axis=0, keepdims=True) for h in hs]
        total = [jnp.sum(f_row[h], axis=1, keepdims=True) for h in hs]
        m_prev = [m_s[h] for h in hs]
        q = [q_ref[0, pl.ds(r0, chunk), h * dqk:(h + 1) * dqk] * scale for h in hs]
        k = [k_ref[0, pl.ds(r0, chunk), h * dqk:(h + 1) * dqk] for h in hs]
        vb = [v_ref[0, pl.ds(r0, chunk), h * dv:(h + 1) * dv].astype(BF16) for h in hs]
        c_prev = [c_s[h] for h in hs]
        n_prev = [n_s[h] for h in hs]
        qkc = [_mm(q[h].astype(BF16),
                   jnp.concatenate([k[h].T.astype(BF16), c_prev[h].astype(BF16)], axis=1))
               for h in hs]
        dmat = [jnp.where(mask, b_col[h] + (i_row[h] - b_row[h]), -jnp.inf) for h in hs]
        inter = [b_col[h] + m_prev[h] for h in hs]
        m_row = [jnp.maximum(inter[h], jnp.max(dmat[h], axis=1, keepdims=True)) for h in hs]
        w_inter = [jnp.exp(inter[h] - m_row[h]) for h in hs]
        sqk = [qkc[h][:, :chunk] * jnp.exp(dmat[h] - m_row[h]) for h in hs]
        dlast = [total[h] + (i_col[h] - b_col[h]) for h in hs]
        m_new = [jnp.maximum(total[h] + m_prev[h], jnp.max(dlast[h], axis=0, keepdims=True))
                 for h in hs]
        kw = [k[h] * jnp.exp(dlast[h] - m_new[h]) for h in hs]
        both = [_mm(jnp.concatenate([sqk[h].astype(BF16), kw[h].T.astype(BF16)], axis=0), vb[h])
                for h in hs]
        for h in hs:
            num = w_inter[h] * qkc[h][:, chunk:] + both[h][:chunk]
            den = (w_inter[h] * jnp.sum(q[h] * n_prev[h], axis=1, keepdims=True)
                   + jnp.sum(sqk[h], axis=1, keepdims=True))
            h_ref[0, pl.ds(r0, chunk), h * dv:(h + 1) * dv] = (
                num / jnp.maximum(jnp.abs(den), jnp.exp(-m_row[h])))
            decay = jnp.exp(total[h] + m_prev[h] - m_new[h])
            c_s[h] = decay * c_prev[h] + both[h][chunk:]
            n_s[h] = decay * n_prev[h] + jnp.sum(kw[h], axis=0, keepdims=True)
            m_s[h] = m_new[h]
        return carry

    lax.fori_loop(0, ncb, step, 0)


def _mlstm_scan(proj, gcol, grow, bcol, brow, *, reverse):
    b, s, _ = proj.shape
    heads = A_HEADS
    dqk = proj.shape[2] // (6 * heads)
    dv = 2 * dqk
    nq, nv = heads * dqk, heads * dv
    blk = min(SCAN_BLOCK, s)
    nb = s // blk
    chunk = gcol.shape[2]
    ncb = blk // chunk
    ng = gcol.shape[3]

    def seq(j):
        return (nb - 1 - j) if reverse else j

    return pl.pallas_call(
        functools.partial(_mlstm_kernel, reverse=reverse, chunk=chunk, heads=heads,
                          dqk=dqk, dv=dv),
        out_shape=jax.ShapeDtypeStruct((b, s, nv), F32),
        grid=(b, nb),
        in_specs=[pl.BlockSpec((1, blk, nq), lambda i, j: (i, seq(j), 0)),
                  pl.BlockSpec((1, blk, nq), lambda i, j: (i, seq(j), 1)),
                  pl.BlockSpec((1, blk, nv), lambda i, j: (i, seq(j), 1)),
                  pl.BlockSpec((1, ncb, chunk, ng), lambda i, j: (i, seq(j), 0, 0)),
                  pl.BlockSpec((1, ncb, ng, chunk), lambda i, j: (i, seq(j), 0, 0)),
                  pl.BlockSpec((1, ng), lambda i, j: (0, 0)),
                  pl.BlockSpec((ng, 1), lambda i, j: (0, 0))],
        out_specs=pl.BlockSpec((1, blk, nv), lambda i, j: (i, seq(j), 0)),
        scratch_shapes=[pltpu.VMEM((heads, dqk, dv), F32),
                        pltpu.VMEM((heads, 1, dqk), F32),
                        pltpu.VMEM((heads, 1, 1), F32)],
        compiler_params=_params("parallel", "arbitrary"),
        name="mlstm_scan_bwd" if reverse else "mlstm_scan_fwd",
    )(proj, proj, proj, gcol, grow, bcol, brow)


def _seq_halo_specs(tm, s, width, col_block=0):
    r = tm // HALO
    last = s // HALO - 1

    def prev(i, j):
        return (i, jnp.maximum(j * r - 1, 0), col_block)

    def nxt(i, j):
        return (i, jnp.minimum((j + 1) * r, last), col_block)

    return [pl.BlockSpec((1, HALO, width), prev),
            pl.BlockSpec((1, tm, width), lambda i, j: (i, j, col_block)),
            pl.BlockSpec((1, HALO, width), nxt)]


def _halo_valid(tm):
    j = pl.program_id(1)
    rid = lax.broadcasted_iota(jnp.int32, (tm + 2 * HALO, 1), 0)
    return jnp.logical_and(jnp.logical_or(rid >= HALO, j > 0),
                           jnp.logical_or(rid < tm + HALO, j < pl.num_programs(1) - 1))


def _dwconv_rows(ext, w_ref, c0, c1, taps, tm):
    n = ext.shape[0]
    acc = None
    for t in range(taps):
        off = t - taps // 2
        src = ext if off == 0 else pltpu.roll(ext, (-off) % n, 0)
        term = src * w_ref[t:t + 1, c0:c1]
        acc = term if acc is None else acc + term
    return acc[HALO:HALO + tm]


def _gdn_prep(proj, conv_w):
    b, s, n = proj.shape
    heads = B_HEADS
    dk = n // (4 * heads)
    width = 3 * heads * dk
    tm = min(ROW_TILE, s)
    specs = []
    for cb in range(3):
        specs += _seq_halo_specs(tm, s, heads * dk, cb)
    return pl.pallas_call(
        functools.partial(_gdn_prep_kernel, tm=tm, heads=heads, dk=dk),
        out_shape=jax.ShapeDtypeStruct((b, s, width), F32),
        grid=(b, s // tm),
        in_specs=specs + [pl.BlockSpec((B_CONV, width), lambda i, j: (0, 0))],
        out_specs=pl.BlockSpec((1, tm, width), lambda i, j: (i, j, 0)),
        compiler_params=_params("parallel", "parallel"),
        name="gdn_prep",
    )(*([proj] * 9), conv_w)


def _gdn_prep_kernel(qp, q, qn, kp, k, kn, vp, v, vn, w_ref, o_ref, *, tm, heads, dk):
    valid = _halo_valid(tm)
    hw = heads * dk
    for part, (p_ref, x_ref, n_ref) in enumerate(((qp, q, qn), (kp, k, kn), (vp, v, vn))):
        for h in range(heads):
            c0, c1 = h * dk, (h + 1) * dk
            ext = jnp.concatenate([p_ref[0, :, c0:c1], x_ref[0, :, c0:c1], n_ref[0, :, c0:c1]], axis=0)
            ext = jnp.where(valid, ext, 0.0)
            y = _dwconv_rows(ext, w_ref, part * hw + c0, part * hw + c1, B_CONV, tm)
            y = y * _sigmoid(y)
            if part < 2:
                y = y * lax.rsqrt(jnp.sum(y * y, axis=-1, keepdims=True) + EPS)
            if part == 0:
                y = y * (dk ** -0.5)
            o_ref[0, :, part * hw + c0:part * hw + c1] = y


def _gdn_kernel(q_ref, k_ref, v_ref, gcol_ref, grow_ref, pcol_ref, prow_ref, o_ref,
                s_s, mq_s, r_s, gl_s, *, reverse, chunk, group, heads, dk):
    @pl.when(pl.program_id(1) == 0)
    def _():
        s_s[...] = jnp.zeros_like(s_s)

    blk = q_ref.shape[1]
    cpg = group // chunk
    clog = chunk.bit_length() - 1
    row = lax.broadcasted_iota(jnp.int32, (group, group), 0)
    col = lax.broadcasted_iota(jnp.int32, (group, group), 1)

    def same_block(log2):
        return jnp.right_shift(row, log2) == jnp.right_shift(col, log2)

    same_chunk = same_block(clog)
    causal = jnp.logical_and(same_chunk, (col >= row) if reverse else (col <= row))
    causal_t = jnp.logical_and(same_chunk, (row >= col) if reverse else (row <= col))
    strict = jnp.logical_and(same_chunk, (col > row) if reverse else (col < row))
    eye = (row == col).astype(F32)
    base_log2 = 3
    same_base = same_block(base_log2)
    level_masks = [jnp.logical_and(same_block(lg + 1), jnp.logical_not(same_block(lg)))
                   for lg in range(base_log2, clog)]
    chunk_of_row = jnp.right_shift(lax.broadcasted_iota(jnp.int32, (group, 1), 0), clog)

    raw_col = gcol_ref[0]
    raw_row = grow_ref[0]
    g_cols = -jnp.exp(pcol_ref[0:1, :]) * _softplus(raw_col[:, :heads] + pcol_ref[1:2, :])
    g_rows = -jnp.exp(prow_ref[:, 0:1]) * _softplus(raw_row[:heads, :] + prow_ref[:, 1:2])
    beta_cols = _sigmoid(raw_col[:, heads:])

    def par(fn, *lists):
        return [fn(*args) for args in zip(*lists)]

    def chunk_local(units):
        rows = [slice(gi * group, (gi + 1) * group) for _, gi in units]
        cols = [slice(h * dk, (h + 1) * dk) for h, _ in units]
        g_col = [g_cols[r, h:h + 1] for (h, _), r in zip(units, rows)]
        g_row = [g_rows[h:h + 1, r] for (h, _), r in zip(units, rows)]
        beta = [beta_cols[r, h:h + 1] for (h, _), r in zip(units, rows)]
        gc_col = par(lambda g: jnp.sum(jnp.where(causal, g, 0.0), axis=1, keepdims=True), g_row)
        gc_row = par(lambda g: jnp.sum(jnp.where(causal_t, g, 0.0), axis=0, keepdims=True), g_col)
        tot_col = par(lambda g: jnp.sum(jnp.where(same_chunk, g, 0.0), axis=1, keepdims=True), g_row)
        decay = par(lambda c, r: jnp.exp(jnp.where(causal, c - r, -jnp.inf)), gc_col, gc_row)
        q = [q_ref[0, r, c] for r, c in zip(rows, cols)]
        k = [k_ref[0, r, c] for r, c in zip(rows, cols)]
        v = [v_ref[0, r, c] for r, c in zip(rows, cols)]
        kb = par(lambda x: x.astype(BF16), k)
        kq = par(lambda kb_, q_: _mm_nt(jnp.concatenate([kb_, q_.astype(BF16)], axis=0), kb_),
                 kb, q)
        a = par(lambda kq_, d, b_: jnp.where(strict, kq_[:group] * d, 0.0) * b_, kq, decay, beta)
        n1 = par(lambda a_: jnp.where(same_base, a_, 0.0), a)
        n2 = par(lambda n: _mm_inv(n, n), n1)
        n4 = par(lambda n: _mm_inv(n, n), n2)
        x = par(lambda n, m: _mm_inv(eye - n, eye + m), n1, n2)
        x = par(lambda x_, n: _mm_inv(x_, eye + n), x, n4)
        for lm in level_masks:
            xl = par(lambda x_, a_: _mm_inv(x_, jnp.where(lm, a_, 0.0)), x, a)
            x = par(lambda x_, xl_: x_ - _mm_inv(xl_, x_), x, xl)
        eg = par(jnp.exp, gc_col)
        wu = par(lambda x_, k_, b_, e, v_: _mm_inv(
            x_, jnp.concatenate([k_ * b_ * e, v_ * b_], axis=1)).astype(BF16), x, k, beta, eg, v)
        qo = par(lambda kq_, d, wu_: _mm((kq_[group:] * d).astype(BF16), wu_), kq, decay, wu)
        kd_by_chunk = par(lambda k_, t_, c_: jnp.concatenate(
            [jnp.where(chunk_of_row == c, k_ * jnp.exp(t_ - c_), 0.0) for c in range(cpg)],
            axis=1).astype(BF16), k, tot_col, gc_col)
        mr = par(_mm_tn, kd_by_chunk, wu)
        for i, (h, gi) in enumerate(units):
            o_ref[0, rows[i], cols[i]] = qo[i][:, dk:]
            q_prime = (q[i] * eg[i] - qo[i][:, :dk]).astype(BF16)
            for c in range(cpg):
                cg = gi * cpg + c
                mq_s[h, cg, 0:dk, :] = mr[i][c * dk:(c + 1) * dk, :dk].astype(BF16)
                mq_s[h, cg, dk:dk + chunk, :] = q_prime[c * chunk:(c + 1) * chunk]
                r_s[h, cg] = mr[i][c * dk:(c + 1) * dk, dk:]
                total = jnp.sum(g_row[i][:, c * chunk:(c + 1) * chunk], axis=1, keepdims=True)
                gl_s[h, cg] = jnp.broadcast_to(jnp.exp(total), (1, dk))

    all_units = [(h, gi) for h in range(heads) for gi in range(blk // group)]
    for u0 in range(0, len(all_units), GDN_UNITS):
        chunk_local(all_units[u0:u0 + GDN_UNITS])

    ncb = blk // chunk
    for ci in range(ncb):
        cc = (ncb - 1 - ci) if reverse else ci
        s_prev = [s_s[h] for h in range(heads)]
        p = [_mm(mq_s[h, cc], s_prev[h].astype(BF16)) for h in range(heads)]
        for h in range(heads):
            s_s[h] = gl_s[h, cc] * s_prev[h] - p[h][:dk] + r_s[h, cc]
            o_ref[0, cc * chunk:(cc + 1) * chunk, h * dk:(h + 1) * dk] += p[h][dk:]


def _gdn_scan(qkv, gcol, grow, pcol, prow, *, reverse):
    b, s, n = qkv.shape
    heads = B_HEADS
    hw = n // 3
    dk = hw // heads
    blk = min(SCAN_BLOCK, s)
    nb = s // blk
    ncb = blk // SCAN_CHUNK
    ng = gcol.shape[2]

    def seq(j):
        return (nb - 1 - j) if reverse else j

    return pl.pallas_call(
        functools.partial(_gdn_kernel, reverse=reverse, chunk=SCAN_CHUNK, group=GDN_GROUP,
                          heads=heads, dk=dk),
        out_shape=jax.ShapeDtypeStruct((b, s, hw), F32),
        grid=(b, nb),
        in_specs=[pl.BlockSpec((1, blk, hw), lambda i, j: (i, seq(j), 0)),
                  pl.BlockSpec((1, blk, hw), lambda i, j: (i, seq(j), 1)),
                  pl.BlockSpec((1, blk, hw), lambda i, j: (i, seq(j), 2)),
                  pl.BlockSpec((1, blk, ng), lambda i, j: (i, seq(j), 0)),
                  pl.BlockSpec((1, ng, blk), lambda i, j: (i, 0, seq(j))),
                  pl.BlockSpec((2, heads), lambda i, j: (0, 0)),
                  pl.BlockSpec((heads, 2), lambda i, j: (0, 0))],
        out_specs=pl.BlockSpec((1, blk, hw), lambda i, j: (i, seq(j), 0)),
        scratch_shapes=[pltpu.VMEM((heads, dk, dk), F32),
                        pltpu.VMEM((heads, ncb, dk + SCAN_CHUNK, dk), BF16),
                        pltpu.VMEM((heads, ncb, dk, dk), F32),
                        pltpu.VMEM((heads, ncb, 1, dk), F32)],
        compiler_params=_params("parallel", "arbitrary"),
        name="gdn_scan_bwd" if reverse else "gdn_scan_fwd",
    )(qkv, qkv, qkv, gcol, grow, pcol, prow)


def _rel_bucket(rel):
    nb = REL_BUCKETS // 2
    exact = nb // 2
    n = jnp.abs(rel)
    large = exact + (jnp.log(jnp.maximum(n, 1).astype(jnp.float32) / exact)
                     / math.log(REL_MAX_DIST / exact) * (nb - exact)).astype(jnp.int32)
    large = jnp.minimum(large, nb - 1)
    return jnp.where(rel > 0, nb, 0) + jnp.where(n < exact, n, large)


def _bias_tiles(rel_bias, t):
    assert t >= REL_MAX_DIST
    n = 2 * t + 1
    m = jnp.arange(n, dtype=jnp.int32)
    rel_in_tile = jnp.where(m < t, m, m - n)
    d = jnp.arange(-2, 3, dtype=jnp.int32)
    rel = d[:, None] * t + rel_in_tile[None, :]
    vals = jnp.transpose(rel_bias.astype(F32)[_rel_bucket(rel)], (2, 0, 1)) * LOG2E
    h = vals.shape[0]
    tiled = jnp.tile(vals, (1, 1, t))[:, :, :t * (n - 1)].reshape(h, 5, t, n - 1)
    return tiled[:, :, :, :t]


def _attn_kernel(q_ref, k_ref, v_ref, bias_ref, lam_ref, ng_ref, o_ref,
                 m1_s, a1_s, m2_s, a2_s, vext_s, *, lambda_init, dh):
    qi = pl.program_id(2)
    ki = pl.program_id(3)
    t = bias_ref.shape[2]
    nq = q_ref.shape[1] // t
    tk, dv = v_ref.shape[1], v_ref.shape[2]
    r = tk // t
    maps = ((m1_s, a1_s), (m2_s, a2_s))

    @pl.when(ki == 0)
    def _():
        for m_s, a_s in maps:
            m_s[...] = jnp.full_like(m_s, -jnp.inf)
            a_s[...] = jnp.zeros_like(a_s)
        vext_s[:, dv:] = jnp.ones((tk, dv), BF16)

    vext_s[:, :dv] = v_ref[0]
    k = k_ref[0]
    first = ki * r - qi * nq

    def update(with_bias):
        def logits(iq):
            rows = slice(iq * t, (iq + 1) * t)
            s = [_mm_nt(q_ref[0, rows, i * dh:(i + 1) * dh], k[:, i * dh:(i + 1) * dh])
                 for i in range(2)]
            if with_bias:
                bias = jnp.concatenate(
                    [bias_ref[0, jnp.clip(first - iq + j, -2, 2) + 2] for j in range(r)], axis=1)
                s = [s_ + bias for s_ in s]
            return s

        s_next = logits(0)
        for iq in range(nq):
            rows = slice(iq * t, (iq + 1) * t)
            s = s_next
            if iq + 1 < nq:
                s_next = logits(iq + 1)
            shift = 0.0 if with_bias else bias_ref[0, jnp.clip(first - iq, -2, 2) + 2, 0:1, 0:1]
            m_prev = [m_s[rows] for m_s, _ in maps]
            m_new = [jnp.maximum(m_prev[i], jnp.max(s[i], axis=1, keepdims=True) + shift)
                     for i in range(2)]
            p = [jnp.exp2(s[i] - (m_new[i] - shift)).astype(BF16) for i in range(2)]
            pv = [_mm(p[i], vext_s[...]) for i in range(2)]
            for i, (m_s, a_s) in enumerate(maps):
                a_s[rows] = jnp.exp2(m_prev[i] - m_new[i]) * a_s[rows] + pv[i]
                m_s[rows] = m_new[i]

    far = jnp.logical_or(first - (nq - 1) >= 2, first + (r - 1) <= -2)

    @pl.when(far)
    def _():
        update(False)

    @pl.when(jnp.logical_not(far))
    def _():
        update(True)

    @pl.when(ki == pl.num_programs(3) - 1)
    def _():
        lp = lam_ref[...]
        lam = (jnp.exp(jnp.sum(lp[0:1] * lp[1:2], axis=1, keepdims=True))
               - jnp.exp(jnp.sum(lp[2:3] * lp[3:4], axis=1, keepdims=True)) + lambda_init)
        a1, a2 = a1_s[...], a2_s[...]
        o = a1[:, :dv] / a1[:, dv:dv + 1] - lam * (a2[:, :dv] / a2[:, dv:dv + 1])
        o_ref[0] = _rms_rows(o) * ng_ref[...] * (1.0 - lambda_init)


def _diff_attn(proj, bias, lam_p, norm_g, lambda_init):
    b, s, n = proj.shape
    heads = C_HEADS
    dv = n // (3 * heads)
    t = bias.shape[2]
    tk = min(ATTN_KEYS, s)
    tq = min(ATTN_ROWS, s)
    return pl.pallas_call(
        functools.partial(_attn_kernel, lambda_init=lambda_init, dh=dv // 2),
        out_shape=jax.ShapeDtypeStruct((b, s, heads * dv), F32),
        grid=(b, heads, s // tq, s // tk),
        in_specs=[pl.BlockSpec((1, tq, dv), lambda i, h, qi, ki: (i, qi, h)),
                  pl.BlockSpec((1, tk, dv), lambda i, h, qi, ki: (i, ki, heads + h)),
                  pl.BlockSpec((1, tk, dv), lambda i, h, qi, ki: (i, ki, 2 * heads + h)),
                  pl.BlockSpec((1, 5, t, t), lambda i, h, qi, ki: (h, 0, 0, 0)),
                  pl.BlockSpec(lam_p.shape, lambda i, h, qi, ki: (0, 0)),
                  pl.BlockSpec((1, dv), lambda i, h, qi, ki: (0, 0))],
        out_specs=pl.BlockSpec((1, tq, dv), lambda i, h, qi, ki: (i, qi, h)),
        scratch_shapes=[pltpu.VMEM((tq, 1), F32), pltpu.VMEM((tq, 2 * dv), F32),
                        pltpu.VMEM((tq, 1), F32), pltpu.VMEM((tq, 2 * dv), F32),
                        pltpu.VMEM((tk, 2 * dv), BF16)],
        compiler_params=_params("parallel", "parallel", "parallel", "arbitrary"),
        name="diff_attn",
    )(proj, proj, proj, bias, lam_p, norm_g.reshape(1, dv))


def _outproj_kernel(*refs, mode, hd):
    if mode == "attn":
        a_ref, w_ref, x_ref, g_ref, o_ref = refs
        act = a_ref[0]
    else:
        a_ref, b_ref, gate_ref, ng_ref, w_ref, x_ref, g_ref, o_ref = refs
        ssum = a_ref[0] + b_ref[0]
        d = ssum.shape[1]
        hs = jnp.concatenate([_rms_rows(ssum[:, c0:c0 + hd]) for c0 in range(0, d, hd)], axis=1)
        hs = hs * ng_ref[...]
        gate = gate_ref[0]
        if mode == "mlstm":
            act = _sigmoid(gate) * hs
        else:
            act = hs * (gate * _sigmoid(gate))
    o_ref[0] = x_ref[0] + g_ref[0] * _mm(act.astype(BF16), w_ref[...])


def _outproj(mode, x, mod, w, a, b=None, gate_src=None, gate_block=0, norm_g=None, hd=0):
    bsz, s, d = x.shape
    tm = min(ROW_TILE, s)
    row = pl.BlockSpec((1, tm, d), lambda i, j: (i, j, 0))
    in_specs, args = [row], [a]
    if mode != "attn":
        in_specs += [row, pl.BlockSpec((1, tm, d), lambda i, j: (i, j, gate_block)),
                     pl.BlockSpec((1, d), lambda i, j: (0, 0))]
        args += [b, gate_src, norm_g.reshape(1, d)]
    in_specs += [pl.BlockSpec((d, d), lambda i, j: (0, 0)), row,
                 pl.BlockSpec((1, 1, d), lambda i, j: (i, 0, 2))]
    args += [w, x, mod]
    return pl.pallas_call(
        functools.partial(_outproj_kernel, mode=mode, hd=hd),
        out_shape=jax.ShapeDtypeStruct((bsz, s, d), F32),
        grid=(bsz, s // tm),
        in_specs=in_specs,
        out_specs=row,
        compiler_params=_params("parallel", "parallel"),
        name="outproj_" + mode,
    )(*args)


def _ffn_kernel(*refs, tm, dff, tn, final):
    if final:
        xp_ref, x_ref, xn_ref, sh_ref, sc_ref, g_ref, wu_ref, cw_ref, cb_ref, wd_ref, fg_ref, o_ref = refs
    else:
        xp_ref, x_ref, xn_ref, sh_ref, sc_ref, g_ref, wu_ref, cw_ref, cb_ref, wd_ref, o_ref = refs
    x = x_ref[0]
    ext = jnp.concatenate([xp_ref[0], x, xn_ref[0]], axis=0)
    hn = _rms_rows(ext) * (1.0 + sc_ref[0]) + sh_ref[0]
    hn = jnp.where(_halo_valid(tm), hn, 0.0).astype(BF16)
    acc = jnp.zeros_like(x)
    for c0 in range(0, dff, tn):
        ua = _dwconv_rows(_mm(hn, wu_ref[:, c0:c0 + tn]), cw_ref, c0, c0 + tn, FFN_CONV, tm)
        ug = _dwconv_rows(_mm(hn, wu_ref[:, dff + c0:dff + c0 + tn]), cw_ref, dff + c0, dff + c0 + tn,
                          FFN_CONV, tm)
        ua = ua + cb_ref[:, c0:c0 + tn]
        ug = ug + cb_ref[:, dff + c0:dff + c0 + tn]
        act = ua * (ug * _sigmoid(ug))
        acc = acc + _mm(act.astype(BF16), wd_ref[c0:c0 + tn, :])
    y = x + g_ref[0] * acc
    if final:
        y = _rms_rows(y) * fg_ref[...]
    o_ref[0] = y


def _conv_ffn(x, mod, w_up, conv_w, conv_b, w_down, final_g=None):
    b, s, d = x.shape
    dff = w_down.shape[0]
    tm = min(ROW_TILE, s)
    final = final_g is not None
    const = lambda i, j: (0, 0)
    in_specs = _seq_halo_specs(tm, s, d) + [
        pl.BlockSpec((1, 1, d), lambda i, j: (i, 0, 3)),
        pl.BlockSpec((1, 1, d), lambda i, j: (i, 0, 4)),
        pl.BlockSpec((1, 1, d), lambda i, j: (i, 0, 5)),
        pl.BlockSpec(w_up.shape, const),
        pl.BlockSpec(conv_w.shape, const),
        pl.BlockSpec((1, 2 * dff), const),
        pl.BlockSpec(w_down.shape, const)]
    args = [x, x, x, mod, mod, mod, w_up, conv_w, conv_b.reshape(1, 2 * dff), w_down]
    if final:
        in_specs.append(pl.BlockSpec((1, d), const))
        args.append(final_g.reshape(1, d))
    return pl.pallas_call(
        functools.partial(_ffn_kernel, tm=tm, dff=dff, tn=FFN_COLS, final=final),
        out_shape=jax.ShapeDtypeStruct((b, s, d), F32),
        grid=(b, s // tm),
        in_specs=in_specs,
        out_specs=pl.BlockSpec((1, tm, d), lambda i, j: (i, j, 0)),
        compiler_params=_params("parallel", "parallel"),
        name="conv_ffn",
    )(*args)


def _mlstm_layer(x, mod, w_in, gate_b, norm_g, w_out):
    heads = A_HEADS
    n_main = w_in.shape[1] - 4 * heads
    proj, gates = _inproj(x, mod, _split_w_in(w_in, n_main), n_main)
    hs = []
    for direction in range(2):
        lo = 2 * heads * direction
        gcol, grow = _gate_layouts(gates, lo, 2 * heads, min(MLSTM_CHUNK, x.shape[1]))
        bias = gate_b[lo:lo + 2 * heads].astype(F32)
        hs.append(_mlstm_scan(proj, gcol, grow, bias.reshape(1, -1), bias.reshape(-1, 1),
                              reverse=direction == 1))
    d = x.shape[2]
    return _outproj("mlstm", x, mod, w_out.astype(BF16), hs[0], hs[1], gate_src=proj,
                    gate_block=n_main // d - 1, norm_g=norm_g, hd=d // heads)


def _gdn_layer(x, mod, w_in, conv_w, a_log, dt_bias, norm_g, w_out):
    heads = B_HEADS
    n_main = w_in.shape[1] - 4 * heads
    proj, gates = _inproj(x, mod, _split_w_in(w_in, n_main), n_main)
    qkv = _gdn_prep(proj, conv_w)
    outs = []
    for direction in range(2):
        lo = 2 * heads * direction
        gcol = gates[:, :, lo:lo + 2 * heads]
        grow = jnp.swapaxes(gcol, 1, 2)
        p = jnp.stack([a_log[direction], dt_bias[direction]]).astype(F32)
        outs.append(_gdn_scan(qkv, gcol, grow, p, p.T, reverse=direction == 1))
    d = x.shape[2]
    return _outproj("gdn", x, mod, w_out.astype(BF16), outs[0], outs[1], gate_src=proj,
                    gate_block=n_main // d - 1, norm_g=jnp.tile(norm_g, heads), hd=d // heads)


def _attn_layer(x, mod, w_in, lam_p, norm_g, w_out, bias, lambda_init):
    n = w_in.shape[1]
    dh = n // (6 * C_HEADS)
    proj = _inproj(x, mod, w_in.astype(BF16), n, out_dtype=BF16, lead_cols=n // 3,
                   lead_scale=dh ** -0.5 * LOG2E)[0]
    o = _diff_attn(proj, bias, lam_p.astype(F32), norm_g, lambda_init)
    return _outproj("attn", x, mod, w_out.astype(BF16), o)


def _trunk(x, mods, bias_tiles, ada_w, ada_b, a_w_in, a_gate_b, a_norm_g, a_w_out, b_w_in, b_conv_w,
           b_a_log, b_dt_bias, b_norm_g, b_w_out, c_w_in, c_lambda, c_norm_g, c_w_out, rel_bias,
           ffn_w_up, ffn_conv_w, ffn_conv_b, ffn_w_down, final_g):
    depth = ada_w.shape[0]
    for i in range(depth):
        mod = mods[i][:, None, :]
        j = i // N_MIXERS
        kind = i % N_MIXERS
        if kind == 0:
            x = _mlstm_layer(x, mod, a_w_in[j], a_gate_b[j], a_norm_g[j], a_w_out[j])
        elif kind == 1:
            x = _gdn_layer(x, mod, b_w_in[j], b_conv_w[j], b_a_log[j], b_dt_bias[j], b_norm_g[j],
                           b_w_out[j])
        else:
            x = _attn_layer(x, mod, c_w_in[j], c_lambda[j], c_norm_g[j], c_w_out[j], bias_tiles,
                            0.8 - 0.6 * math.exp(-0.3 * i))
        x = _conv_ffn(x, mod, ffn_w_up[i].astype(BF16), ffn_conv_w[i], ffn_conv_b[i],
                      ffn_w_down[i].astype(BF16), final_g if i == depth - 1 else None)
    return x


def kernel(x_prompt, x_sample, c_prompt, c_sample, ada_w, ada_b, a_w_in, a_gate_b, a_norm_g, a_w_out, b_w_in, b_conv_w, b_a_log, b_dt_bias, b_norm_g, b_w_out, c_w_in, c_lambda, c_norm_g, c_w_out, rel_bias, ffn_w_up, ffn_conv_w, ffn_conv_b, ffn_w_down, final_g):
    weights = (ada_w, ada_b, a_w_in, a_gate_b, a_norm_g, a_w_out, b_w_in, b_conv_w, b_a_log,
               b_dt_bias, b_norm_g, b_w_out, c_w_in, c_lambda, c_norm_g, c_w_out, rel_bias,
               ffn_w_up, ffn_conv_w, ffn_conv_b, ffn_w_down, final_g)
    nb_p, nb_s = c_prompt.shape[0], c_sample.shape[0]
    c_all = jnp.concatenate([c_prompt, c_sample], axis=0)
    pad = (-c_all.shape[0]) % SUBLANES
    mods = _ada_mod(jnp.pad(c_all, ((0, pad), (0, 0))), ada_w, ada_b)
    outs, bias_tiles = [], {}
    for x, lo, n in ((x_prompt, 0, nb_p), (x_sample, nb_p, nb_s)):
        t = min(ATTN_TILE, x.shape[1])
        if t not in bias_tiles:
            bias_tiles[t] = _bias_tiles(rel_bias, t)
        outs.append(_trunk(x, mods[:, lo:lo + n], bias_tiles[t], *weights))
    return tuple(outs)
```

```python
import functools
import math

import jax
import jax.numpy as jnp
from jax import lax
from jax.experimental import pallas as pl
from jax.experimental.pallas import tpu as pltpu

F32 = jnp.float32
BF16 = jnp.bfloat16
EPS = 1e-6
LOG2E = math.log2(math.e)

N_MIXERS = 3
A_HEADS = 4
B_HEADS = 8
B_CONV = 5
C_HEADS = 8
REL_BUCKETS = 32
REL_MAX_DIST = 128
FFN_CONV = 3

LANES = 128
SUBLANES = 8
HALO = SUBLANES
VMEM_LIMIT = 56 * 1024 * 1024

ROW_TILE = 512
SCAN_BLOCK = 512
SCAN_CHUNK = 64
GDN_GROUP = 256
GDN_UNITS = 4
MLSTM_CHUNK = 256
ATTN_TILE = 512
ATTN_ROWS = 2048
ATTN_KEYS = 2048
FFN_COLS = 1408


def _params(*sem):
    return pltpu.CompilerParams(dimension_semantics=sem, vmem_limit_bytes=VMEM_LIMIT)


def _sigmoid(x):
    return 1.0 / (1.0 + jnp.exp(-x))


def _softplus(x):
    return jnp.maximum(x, 0.0) + jnp.log(1.0 + jnp.exp(-jnp.abs(x)))


def _rms_rows(x):
    return x * lax.rsqrt(jnp.mean(x * x, axis=-1, keepdims=True) + EPS)


def _mm(a, b):
    return jnp.dot(a, b, preferred_element_type=F32)


def _mm_inv(a, b):
    return _mm(a.astype(BF16), b.astype(BF16))


def _mm_nt(a, b):
    return lax.dot_general(a, b, (((1,), (1,)), ((), ())), preferred_element_type=F32)


def _mm_tn(a, b):
    return lax.dot_general(a, b, (((0,), (0,)), ((), ())), preferred_element_type=F32)


def _mod_kernel(c_ref, w_ref, b_ref, o_ref):
    c = c_ref[...]
    cs = c * _sigmoid(c)
    o_ref[0] = _mm(cs.astype(BF16), w_ref[0].astype(BF16)) + b_ref[0]


def _ada_mod(c, ada_w, ada_b):
    depth, d, n = ada_w.shape
    bp = c.shape[0]
    tn = 1536
    return pl.pallas_call(
        _mod_kernel,
        out_shape=jax.ShapeDtypeStruct((depth, bp, n), F32),
        grid=(depth, n // tn),
        in_specs=[pl.BlockSpec((bp, d), lambda l, j: (0, 0)),
                  pl.BlockSpec((1, d, tn), lambda l, j: (l, 0, j)),
                  pl.BlockSpec((1, 1, tn), lambda l, j: (l, 0, j))],
        out_specs=pl.BlockSpec((1, bp, tn), lambda l, j: (l, 0, j)),
        compiler_params=_params("parallel", "parallel"),
        name="ada_mod",
    )(c, ada_w, ada_b.reshape(depth, 1, n))


def _inproj_kernel(x_ref, sh_ref, sc_ref, w_ref, *o_refs, n_main, tn, lead_cols, lead_scale):
    hn = (_rms_rows(x_ref[0]) * (1.0 + sc_ref[0]) + sh_ref[0]).astype(BF16)
    for c0 in range(0, n_main, tn):
        y = _mm(hn, w_ref[:, c0:c0 + tn])
        if c0 < lead_cols:
            y = y * lead_scale
        o_refs[0][0, :, c0:c0 + tn] = y.astype(o_refs[0].dtype)
    if len(o_refs) > 1:
        o_refs[1][0] = _mm(hn, w_ref[:, n_main:])


def _inproj(x, mod, w, n_main, out_dtype=F32, lead_cols=0, lead_scale=1.0):
    b, s, d = x.shape
    n_all = w.shape[1]
    tm = min(ROW_TILE, s)
    tn = 512
    assert lead_cols % tn == 0
    out_shape = [jax.ShapeDtypeStruct((b, s, n_main), out_dtype)]
    out_specs = [pl.BlockSpec((1, tm, n_main), lambda i, j: (i, j, 0))]
    if n_all > n_main:
        out_shape.append(jax.ShapeDtypeStruct((b, s, n_all - n_main), F32))
        out_specs.append(pl.BlockSpec((1, tm, n_all - n_main), lambda i, j: (i, j, 0)))
    return pl.pallas_call(
        functools.partial(_inproj_kernel, n_main=n_main, tn=tn, lead_cols=lead_cols,
                          lead_scale=lead_scale),
        out_shape=out_shape,
        grid=(b, s // tm),
        in_specs=[pl.BlockSpec((1, tm, d), lambda i, j: (i, j, 0)),
                  pl.BlockSpec((1, 1, d), lambda i, j: (i, 0, 0)),
                  pl.BlockSpec((1, 1, d), lambda i, j: (i, 0, 1)),
                  pl.BlockSpec((d, n_all), lambda i, j: (0, 0))],
        out_specs=out_specs,
        compiler_params=_params("parallel", "parallel"),
        name="inproj",
    )(x, mod, mod, w)


def _split_w_in(w, n_main):
    d, n = w.shape
    if n == n_main:
        return w.astype(BF16)
    pad = LANES - (n - n_main)
    return jnp.pad(w, ((0, 0), (0, pad))).astype(BF16)


def _gate_layouts(g, lo, width, chunk):
    b, s, _ = g.shape
    col = g[:, :, lo:lo + width].reshape(b, s // chunk, chunk, width)
    return col, jnp.swapaxes(col, 2, 3)


def _mlstm_kernel(q_ref, k_ref, v_ref, gcol_ref, grow_ref, bcol_ref, brow_ref, h_ref,
                  c_s, n_s, m_s, *, reverse, chunk, heads, dqk, dv):
    @pl.when(pl.program_id(1) == 0)
    def _():
        c_s[...] = jnp.zeros_like(c_s)
        n_s[...] = jnp.zeros_like(n_s)
        m_s[...] = jnp.zeros_like(m_s)

    ncb = gcol_ref.shape[1]
    row = lax.broadcasted_iota(jnp.int32, (chunk, chunk), 0)
    col = lax.broadcasted_iota(jnp.int32, (chunk, chunk), 1)
    mask = (col >= row) if reverse else (col <= row)
    mask_t = (row >= col) if reverse else (row <= col)
    scale = dqk ** -0.5

    def step(ci, carry):
        cc = (ncb - 1 - ci) if reverse else ci
        r0 = pl.multiple_of(cc * chunk, chunk)
        gcol = gcol_ref[0, cc] + bcol_ref[...]
        grow = grow_ref[0, cc] + brow_ref[...]
        fcol = -_softplus(-gcol)
        frow = -_softplus(-grow)
        hs = range(heads)
        i_col = [gcol[:, h:h + 1] for h in hs]
        f_col = [fcol[:, heads + h:heads + h + 1] for h in hs]
        i_row = [grow[h:h + 1, :] for h in hs]
        f_row = [frow[heads + h:heads + h + 1, :] for h in hs]
        b_col = [jnp.sum(jnp.where(mask, f_row[h], 0.0), axis=1, keepdims=True) for h in hs]
        b_row = [jnp.sum(jnp.where(mask_t, f_col[h], 0.0), axis=0, keepdims=True) for h in hs]
        total = [jnp.sum(f_row[h], axis=1, keepdims=True) for h in hs]
        m_prev = [m_s[h] for h in hs]
        q = [q_ref[0, pl.ds(r0, chunk), h * dqk:(h + 1) * dqk] * scale for h in hs]
        k = [k_ref[0, pl.ds(r0, chunk), h * dqk:(h + 1) * dqk] for h in hs]
        vb = [v_ref[0, pl.ds(r0, chunk), h * dv:(h + 1) * dv].astype(BF16) for h in hs]
        c_prev = [c_s[h] for h in hs]
        n_prev = [n_s[h] for h in hs]
        qkc = [_mm(q[h].astype(BF16),
                   jnp.concatenate([k[h].T.astype(BF16), c_prev[h].astype(BF16)], axis=1))
               for h in hs]
        dmat = [jnp.where(mask, b_col[h] + (i_row[h] - b_row[h]), -jnp.inf) for h in hs]
        inter = [b_col[h] + m_prev[h] for h in hs]
        m_row = [jnp.maximum(inter[h], jnp.max(dmat[h], axis=1, keepdims=True)) for h in hs]
        w_inter = [jnp.exp(inter[h] - m_row[h]) for h in hs]
        sqk = [qkc[h][:, :chunk] * jnp.exp(dmat[h] - m_row[h]) for h in hs]
        dlast = [total[h] + (i_col[h] - b_col[h]) for h in hs]
        m_new = [jnp.maximum(total[h] + m_prev[h], jnp.max(dlast[h], axis=0, keepdims=True))
                 for h in hs]
        kw = [k[h] * jnp.exp(dlast[h] - m_new[h]) for h in hs]
        both = [_mm(jnp.concatenate([sqk[h].astype(BF16), kw[h].T.astype(BF16)], axis=0), vb[h])
                for h in hs]
        for h in hs:
            num = w_inter[h] * qkc[h][:, chunk:] + both[h][:chunk]
            den = (w_inter[h] * jnp.sum(q[h] * n_prev[h], axis=1, keepdims=True)
                   + jnp.sum(sqk[h], axis=1, keepdims=True))
            h_ref[0, pl.ds(r0, chunk), h * dv:(h + 1) * dv] = (
                num / jnp.maximum(jnp.abs(den), jnp.exp(-m_row[h])))
            decay = jnp.exp(total[h] + m_prev[h] - m_new[h])
            c_s[h] = decay * c_prev[h] + both[h][chunk:]
            n_s[h] = decay * n_prev[h] + jnp.sum(kw[h], axis=0, keepdims=True)
            m_s[h] = m_new[h]
        return carry

    lax.fori_loop(0, ncb, step, 0)


def _mlstm_scan(proj, gcol, grow, bcol, brow, *, reverse):
    b, s, _ = proj.shape
    heads = A_HEADS
    dqk = proj.shape[2] // (6 * heads)
    dv = 2 * dqk
    nq, nv = heads * dqk, heads * dv
    blk = min(SCAN_BLOCK, s)
    nb = s // blk
    chunk = gcol.shape[2]
    ncb = blk // chunk
    ng = gcol.shape[3]

    def seq(j):
        return (nb - 1 - j) if reverse else j

    return pl.pallas_call(
        functools.partial(_mlstm_kernel, reverse=reverse, chunk=chunk, heads=heads,
                          dqk=dqk, dv=dv),
        out_shape=jax.ShapeDtypeStruct((b, s, nv), F32),
        grid=(b, nb),
        in_specs=[pl.BlockSpec((1, blk, nq), lambda i, j: (i, seq(j), 0)),
                  pl.BlockSpec((1, blk, nq), lambda i, j: (i, seq(j), 1)),
                  pl.BlockSpec((1, blk, nv), lambda i, j: (i, seq(j), 1)),
                  pl.BlockSpec((1, ncb, chunk, ng), lambda i, j: (i, seq(j), 0, 0)),
                  pl.BlockSpec((1, ncb, ng, chunk), lambda i, j: (i, seq(j), 0, 0)),
                  pl.BlockSpec((1, ng), lambda i, j: (0, 0)),
                  pl.BlockSpec((ng, 1), lambda i, j: (0, 0))],
        out_specs=pl.BlockSpec((1, blk, nv), lambda i, j: (i, seq(j), 0)),
        scratch_shapes=[pltpu.VMEM((heads, dqk, dv), F32),
                        pltpu.VMEM((heads, 1, dqk), F32),
                        pltpu.VMEM((heads, 1, 1), F32)],
        compiler_params=_params("parallel", "arbitrary"),
        name="mlstm_scan_bwd" if reverse else "mlstm_scan_fwd",
    )(proj, proj, proj, gcol, grow, bcol, brow)


def _seq_halo_specs(tm, s, width, col_block=0):
    r = tm // HALO
    last = s // HALO - 1

    def prev(i, j):
        return (i, jnp.maximum(j * r - 1, 0), col_block)

    def nxt(i, j):
        return (i, jnp.minimum((j + 1) * r, last), col_block)

    return [pl.BlockSpec((1, HALO, width), prev),
            pl.BlockSpec((1, tm, width), lambda i, j: (i, j, col_block)),
            pl.BlockSpec((1, HALO, width), nxt)]


def _halo_valid(tm):
    j = pl.program_id(1)
    rid = lax.broadcasted_iota(jnp.int32, (tm + 2 * HALO, 1), 0)
    return jnp.logical_and(jnp.logical_or(rid >= HALO, j > 0),
                           jnp.logical_or(rid < tm + HALO, j < pl.num_programs(1) - 1))


def _dwconv_rows(ext, w_ref, c0, c1, taps, tm):
    n = ext.shape[0]
    acc = None
    for t in range(taps):
        off = t - taps // 2
        src = ext if off == 0 else pltpu.roll(ext, (-off) % n, 0)
        term = src * w_ref[t:t + 1, c0:c1]
        acc = term if acc is None else acc + term
    return acc[HALO:HALO + tm]


def _gdn_inproj_kernel(xp_ref, x_ref, xn_ref, sh_ref, sc_ref, w_ref, cw_ref, qkv_ref, z_ref, g_ref,
                       *, tm, heads, dk, tn):
    ext = jnp.concatenate([xp_ref[0], x_ref[0], xn_ref[0]], axis=0)
    hn = _rms_rows(ext) * (1.0 + sc_ref[0]) + sh_ref[0]
    hn = jnp.where(_halo_valid(tm), hn, 0.0).astype(BF16)
    hw = heads * dk
    for c0 in range(0, 3 * hw, tn):
        y = _dwconv_rows(_mm(hn, w_ref[:, c0:c0 + tn]), cw_ref, c0, c0 + tn, B_CONV, tm)
        y = y * _sigmoid(y)
        for g0 in range(0, tn, dk):
            yh = y[:, g0:g0 + dk]
            if c0 < 2 * hw:
                yh = yh * lax.rsqrt(jnp.sum(yh * yh, axis=-1, keepdims=True) + EPS)
            if c0 < hw:
                yh = yh * (dk ** -0.5)
            qkv_ref[0, :, c0 + g0:c0 + g0 + dk] = yh
    hm = hn[HALO:HALO + tm]
    for c0 in range(0, hw, tn):
        z_ref[0, :, c0:c0 + tn] = _mm(hm, w_ref[:, 3 * hw + c0:3 * hw + c0 + tn])
    g_ref[0] = _mm(hm, w_ref[:, 4 * hw:])


def _gdn_inproj(x, mod, w, conv_w):
    b, s, d = x.shape
    heads = B_HEADS
    n_gate = LANES
    hw = (w.shape[1] - n_gate) // 4
    tm = min(ROW_TILE, s)
    const = lambda i, j: (0, 0)
    row = lambda n: pl.BlockSpec((1, tm, n), lambda i, j: (i, j, 0))
    return pl.pallas_call(
        functools.partial(_gdn_inproj_kernel, tm=tm, heads=heads, dk=hw // heads, tn=512),
        out_shape=[jax.ShapeDtypeStruct((b, s, 3 * hw), F32),
                   jax.ShapeDtypeStruct((b, s, hw), F32),
                   jax.ShapeDtypeStruct((b, s, n_gate), F32)],
        grid=(b, s // tm),
        in_specs=_seq_halo_specs(tm, s, d) + [
            pl.BlockSpec((1, 1, d), lambda i, j: (i, 0, 0)),
            pl.BlockSpec((1, 1, d), lambda i, j: (i, 0, 1)),
            pl.BlockSpec(w.shape, const),
            pl.BlockSpec(conv_w.shape, const)],
        out_specs=[row(3 * hw), row(hw), row(n_gate)],
        compiler_params=_params("parallel", "parallel"),
        name="gdn_inproj",
    )(x, x, x, mod, mod, w, conv_w)


def _gdn_kernel(q_ref, k_ref, v_ref, gcol_ref, grow_ref, pcol_ref, prow_ref, o_ref,
                s_s, mq_s, r_s, gl_s, *, reverse, chunk, group, heads, dk):
    @pl.when(pl.program_id(1) == 0)
    def _():
        s_s[...] = jnp.zeros_like(s_s)

    blk = q_ref.shape[1]
    cpg = group // chunk
    clog = chunk.bit_length() - 1
    row = lax.broadcasted_iota(jnp.int32, (group, group), 0)
    col = lax.broadcasted_iota(jnp.int32, (group, group), 1)

    def same_block(log2):
        return jnp.right_shift(row, log2) == jnp.right_shift(col, log2)

    same_chunk = same_block(clog)
    causal = jnp.logical_and(same_chunk, (col >= row) if reverse else (col <= row))
    causal_t = jnp.logical_and(same_chunk, (row >= col) if reverse else (row <= col))
    strict = jnp.logical_and(same_chunk, (col > row) if reverse else (col < row))
    eye = (row == col).astype(F32)
    base_log2 = 3
    same_base = same_block(base_log2)
    level_masks = [jnp.logical_and(same_block(lg + 1), jnp.logical_not(same_block(lg)))
                   for lg in range(base_log2, clog)]
    chunk_of_row = jnp.right_shift(lax.broadcasted_iota(jnp.int32, (group, 1), 0), clog)

    raw_col = gcol_ref[0]
    raw_row = grow_ref[0]
    g_cols = -jnp.exp(pcol_ref[0:1, :]) * _softplus(raw_col[:, :heads] + pcol_ref[1:2, :])
    g_rows = -jnp.exp(prow_ref[:, 0:1]) * _softplus(raw_row[:heads, :] + prow_ref[:, 1:2])
    beta_cols = _sigmoid(raw_col[:, heads:])

    def par(fn, *lists):
        return [fn(*args) for args in zip(*lists)]

    def chunk_local(units):
        rows = [slice(gi * group, (gi + 1) * group) for _, gi in units]
        cols = [slice(h * dk, (h + 1) * dk) for h, _ in units]
        g_col = [g_cols[r, h:h + 1] for (h, _), r in zip(units, rows)]
        g_row = [g_rows[h:h + 1, r] for (h, _), r in zip(units, rows)]
        beta = [beta_cols[r, h:h + 1] for (h, _), r in zip(units, rows)]
        gc_col = par(lambda g: jnp.sum(jnp.where(causal, g, 0.0), axis=1, keepdims=True), g_row)
        gc_row = par(lambda g: jnp.sum(jnp.where(causal_t, g, 0.0), axis=0, keepdims=True), g_col)
        tot_col = par(lambda g: jnp.sum(jnp.where(same_chunk, g, 0.0), axis=1, keepdims=True), g_row)
        decay = par(lambda c, r: jnp.exp(jnp.where(causal, c - r, -jnp.inf)), gc_col, gc_row)
        q = [q_ref[0, r, c] for r, c in zip(rows, cols)]
        k = [k_ref[0, r, c] for r, c in zip(rows, cols)]
        v = [v_ref[0, r, c] for r, c in zip(rows, cols)]
        kb = par(lambda x: x.astype(BF16), k)
        kq = par(lambda kb_, q_: _mm_nt(jnp.concatenate([kb_, q_.astype(BF16)], axis=0), kb_),
                 kb, q)
        a = par(lambda kq_, d, b_: jnp.where(strict, kq_[:group] * d, 0.0) * b_, kq, decay, beta)
        n1 = par(lambda a_: jnp.where(same_base, a_, 0.0), a)
        n2 = par(lambda n: _mm_inv(n, n), n1)
        n4 = par(lambda n: _mm_inv(n, n), n2)
        x = par(lambda n, m: _mm_inv(eye - n, eye + m), n1, n2)
        x = par(lambda x_, n: _mm_inv(x_, eye + n), x, n4)
        for lm in level_masks:
            xl = par(lambda x_, a_: _mm_inv(x_, jnp.where(lm, a_, 0.0)), x, a)
            x = par(lambda x_, xl_: x_ - _mm_inv(xl_, x_), x, xl)
        eg = par(jnp.exp, gc_col)
        wu = par(lambda x_, k_, b_, e, v_: _mm_inv(
            x_, jnp.concatenate([k_ * b_ * e, v_ * b_], axis=1)).astype(BF16), x, k, beta, eg, v)
        qo = par(lambda kq_, d, wu_: _mm((kq_[group:] * d).astype(BF16), wu_), kq, decay, wu)
        kd_by_chunk = par(lambda k_, t_, c_: jnp.concatenate(
            [jnp.where(chunk_of_row == c, k_ * jnp.exp(t_ - c_), 0.0) for c in range(cpg)],
            axis=1).astype(BF16), k, tot_col, gc_col)
        mr = par(_mm_tn, kd_by_chunk, wu)
        for i, (h, gi) in enumerate(units):
            o_ref[0, rows[i], cols[i]] = qo[i][:, dk:]
            q_prime = (q[i] * eg[i] - qo[i][:, :dk]).astype(BF16)
            for c in range(cpg):
                cg = gi * cpg + c
                mq_s[h, cg, 0:dk, :] = mr[i][c * dk:(c + 1) * dk, :dk].astype(BF16)
                mq_s[h, cg, dk:dk + chunk, :] = q_prime[c * chunk:(c + 1) * chunk]
                r_s[h, cg] = mr[i][c * dk:(c + 1) * dk, dk:]
                total = jnp.sum(g_row[i][:, c * chunk:(c + 1) * chunk], axis=1, keepdims=True)
                gl_s[h, cg] = jnp.broadcast_to(jnp.exp(total), (1, dk))

    all_units = [(h, gi) for h in range(heads) for gi in range(blk // group)]
    for u0 in range(0, len(all_units), GDN_UNITS):
        chunk_local(all_units[u0:u0 + GDN_UNITS])

    ncb = blk // chunk
    for ci in range(ncb):
        cc = (ncb - 1 - ci) if reverse else ci
        s_prev = [s_s[h] for h in range(heads)]
        p = [_mm(mq_s[h, cc], s_prev[h].astype(BF16)) for h in range(heads)]
        for h in range(heads):
            s_s[h] = gl_s[h, cc] * s_prev[h] - p[h][:dk] + r_s[h, cc]
            o_ref[0, cc * chunk:(cc + 1) * chunk, h * dk:(h + 1) * dk] += p[h][dk:]


def _gdn_scan(qkv, gcol, grow, pcol, prow, *, reverse):
    b, s, n = qkv.shape
    heads = B_HEADS
    hw = n // 3
    dk = hw // heads
    blk = min(SCAN_BLOCK, s)
    nb = s // blk
    ncb = blk // SCAN_CHUNK
    ng = gcol.shape[2]

    def seq(j):
        return (nb - 1 - j) if reverse else j

    return pl.pallas_call(
        functools.partial(_gdn_kernel, reverse=reverse, chunk=SCAN_CHUNK, group=GDN_GROUP,
                          heads=heads, dk=dk),
        out_shape=jax.ShapeDtypeStruct((b, s, hw), F32),
        grid=(b, nb),
        in_specs=[pl.BlockSpec((1, blk, hw), lambda i, j: (i, seq(j), 0)),
                  pl.BlockSpec((1, blk, hw), lambda i, j: (i, seq(j), 1)),
                  pl.BlockSpec((1, blk, hw), lambda i, j: (i, seq(j), 2)),
                  pl.BlockSpec((1, blk, ng), lambda i, j: (i, seq(j), 0)),
                  pl.BlockSpec((1, ng, blk), lambda i, j: (i, 0, seq(j))),
                  pl.BlockSpec((2, heads), lambda i, j: (0, 0)),
                  pl.BlockSpec((heads, 2), lambda i, j: (0, 0))],
        out_specs=pl.BlockSpec((1, blk, hw), lambda i, j: (i, seq(j), 0)),
        scratch_shapes=[pltpu.VMEM((heads, dk, dk), F32),
                        pltpu.VMEM((heads, ncb, dk + SCAN_CHUNK, dk), BF16),
                        pltpu.VMEM((heads, ncb, dk, dk), F32),
                        pltpu.VMEM((heads, ncb, 1, dk), F32)],
        compiler_params=_params("parallel", "arbitrary"),
        name="gdn_scan_bwd" if reverse else "gdn_scan_fwd",
    )(qkv, qkv, qkv, gcol, grow, pcol, prow)


def _rel_bucket(rel):
    nb = REL_BUCKETS // 2
    exact = nb // 2
    n = jnp.abs(rel)
    large = exact + (jnp.log(jnp.maximum(n, 1).astype(jnp.float32) / exact)
                     / math.log(REL_MAX_DIST / exact) * (nb - exact)).astype(jnp.int32)
    large = jnp.minimum(large, nb - 1)
    return jnp.where(rel > 0, nb, 0) + jnp.where(n < exact, n, large)


def _bias_tiles(rel_bias, t):
    assert t >= REL_MAX_DIST
    n = 2 * t + 1
    m = jnp.arange(n, dtype=jnp.int32)
    rel_in_tile = jnp.where(m < t, m, m - n)
    d = jnp.arange(-2, 3, dtype=jnp.int32)
    rel = d[:, None] * t + rel_in_tile[None, :]
    vals = jnp.transpose(rel_bias.astype(F32)[_rel_bucket(rel)], (2, 0, 1)) * LOG2E
    h = vals.shape[0]
    tiled = jnp.tile(vals, (1, 1, t))[:, :, :t * (n - 1)].reshape(h, 5, t, n - 1)
    return tiled[:, :, :, :t]


def _attn_kernel(q_ref, k_ref, v_ref, bias_ref, lam_ref, ng_ref, o_ref,
                 m1_s, a1_s, m2_s, a2_s, vext_s, *, lambda_init, dh):
    qi = pl.program_id(2)
    ki = pl.program_id(3)
    t = bias_ref.shape[2]
    nq = q_ref.shape[1] // t
    tk, dv = v_ref.shape[1], v_ref.shape[2]
    r = tk // t
    maps = ((m1_s, a1_s), (m2_s, a2_s))

    @pl.when(ki == 0)
    def _():
        for m_s, a_s in maps:
            m_s[...] = jnp.full_like(m_s, -jnp.inf)
            a_s[...] = jnp.zeros_like(a_s)
        vext_s[:, dv:] = jnp.ones((tk, dv), BF16)

    vext_s[:, :dv] = v_ref[0]
    k_maps = [k_ref[0, :, i * dh:(i + 1) * dh] for i in range(2)]
    first = ki * r - qi * nq

    def update(with_bias):
        def logits(iq):
            rows = slice(iq * t, (iq + 1) * t)
            s = [_mm_nt(q_ref[0, rows, i * dh:(i + 1) * dh], k_maps[i]) for i in range(2)]
            if with_bias:
                bias = jnp.concatenate(
                    [bias_ref[0, jnp.clip(first - iq + j, -2, 2) + 2] for j in range(r)], axis=1)
                s = [s_ + bias for s_ in s]
            return s

        s_next = logits(0)
        for iq in range(nq):
            rows = slice(iq * t, (iq + 1) * t)
            s = s_next
            if iq + 1 < nq:
                s_next = logits(iq + 1)
            shift = 0.0 if with_bias else bias_ref[0, jnp.clip(first - iq, -2, 2) + 2, 0:1, 0:1]
            m_prev = [m_s[rows] for m_s, _ in maps]
            m_new = [jnp.maximum(m_prev[i], jnp.max(s[i], axis=1, keepdims=True) + shift)
                     for i in range(2)]
            p = [jnp.exp2(s[i] - (m_new[i] - shift)).astype(BF16) for i in range(2)]
            pv = [_mm(p[i], vext_s[...]) for i in range(2)]
            for i, (m_s, a_s) in enumerate(maps):
                a_s[rows] = jnp.exp2(m_prev[i] - m_new[i]) * a_s[rows] + pv[i]
                m_s[rows] = m_new[i]

    far = jnp.logical_or(first - (nq - 1) >= 2, first + (r - 1) <= -2)

    @pl.when(far)
    def _():
        update(False)

    @pl.when(jnp.logical_not(far))
    def _():
        update(True)

    @pl.when(ki == pl.num_programs(3) - 1)
    def _():
        lp = lam_ref[...]
        lam = (jnp.exp(jnp.sum(lp[0:1] * lp[1:2], axis=1, keepdims=True))
               - jnp.exp(jnp.sum(lp[2:3] * lp[3:4], axis=1, keepdims=True)) + lambda_init)
        a1, a2 = a1_s[...], a2_s[...]
        o = a1[:, :dv] / a1[:, dv:dv + 1] - lam * (a2[:, :dv] / a2[:, dv:dv + 1])
        o_ref[0] = _rms_rows(o) * ng_ref[...] * (1.0 - lambda_init)


def _diff_attn(proj, bias, lam_p, norm_g, lambda_init):
    b, s, n = proj.shape
    heads = C_HEADS
    dv = n // (3 * heads)
    t = bias.shape[2]
    tk = min(ATTN_KEYS, s)
    tq = min(ATTN_ROWS, s)
    return pl.pallas_call(
        functools.partial(_attn_kernel, lambda_init=lambda_init, dh=dv // 2),
        out_shape=jax.ShapeDtypeStruct((b, s, heads * dv), F32),
        grid=(b, heads, s // tq, s // tk),
        in_specs=[pl.BlockSpec((1, tq, dv), lambda i, h, qi, ki: (i, qi, h)),
                  pl.BlockSpec((1, tk, dv), lambda i, h, qi, ki: (i, ki, heads + h)),
                  pl.BlockSpec((1, tk, dv), lambda i, h, qi, ki: (i, ki, 2 * heads + h)),
                  pl.BlockSpec((1, 5, t, t), lambda i, h, qi, ki: (h, 0, 0, 0)),
                  pl.BlockSpec(lam_p.shape, lambda i, h, qi, ki: (0, 0)),
                  pl.BlockSpec((1, dv), lambda i, h, qi, ki: (0, 0))],
        out_specs=pl.BlockSpec((1, tq, dv), lambda i, h, qi, ki: (i, qi, h)),
        scratch_shapes=[pltpu.VMEM((tq, 1), F32), pltpu.VMEM((tq, 2 * dv), F32),
                        pltpu.VMEM((tq, 1), F32), pltpu.VMEM((tq, 2 * dv), F32),
                        pltpu.VMEM((tk, 2 * dv), BF16)],
        compiler_params=_params("parallel", "parallel", "parallel", "arbitrary"),
        name="diff_attn",
    )(proj, proj, proj, bias, lam_p, norm_g.reshape(1, dv))


def _outproj_kernel(*refs, mode, hd):
    if mode == "attn":
        a_ref, w_ref, x_ref, g_ref, o_ref = refs
        act = a_ref[0]
    else:
        a_ref, b_ref, gate_ref, ng_ref, w_ref, x_ref, g_ref, o_ref = refs
        ssum = a_ref[0] + b_ref[0]
        d = ssum.shape[1]
        hs = jnp.concatenate([_rms_rows(ssum[:, c0:c0 + hd]) for c0 in range(0, d, hd)], axis=1)
        hs = hs * ng_ref[...]
        gate = gate_ref[0]
        if mode == "mlstm":
            act = _sigmoid(gate) * hs
        else:
            act = hs * (gate * _sigmoid(gate))
    o_ref[0] = x_ref[0] + g_ref[0] * _mm(act.astype(BF16), w_ref[...])


def _outproj(mode, x, mod, w, a, b=None, gate_src=None, gate_block=0, norm_g=None, hd=0):
    bsz, s, d = x.shape
    tm = min(ROW_TILE, s)
    row = pl.BlockSpec((1, tm, d), lambda i, j: (i, j, 0))
    in_specs, args = [row], [a]
    if mode != "attn":
        in_specs += [row, pl.BlockSpec((1, tm, d), lambda i, j: (i, j, gate_block)),
                     pl.BlockSpec((1, d), lambda i, j: (0, 0))]
        args += [b, gate_src, norm_g.reshape(1, d)]
    in_specs += [pl.BlockSpec((d, d), lambda i, j: (0, 0)), row,
                 pl.BlockSpec((1, 1, d), lambda i, j: (i, 0, 2))]
    args += [w, x, mod]
    return pl.pallas_call(
        functools.partial(_outproj_kernel, mode=mode, hd=hd),
        out_shape=jax.ShapeDtypeStruct((bsz, s, d), F32),
        grid=(bsz, s // tm),
        in_specs=in_specs,
        out_specs=row,
        compiler_params=_params("parallel", "parallel"),
        name="outproj_" + mode,
    )(*args)


def _ffn_kernel(*refs, tm, dff, tn, final):
    if final:
        xp_ref, x_ref, xn_ref, sh_ref, sc_ref, g_ref, wu_ref, cw_ref, cb_ref, wd_ref, fg_ref, o_ref = refs
    else:
        xp_ref, x_ref, xn_ref, sh_ref, sc_ref, g_ref, wu_ref, cw_ref, cb_ref, wd_ref, o_ref = refs
    x = x_ref[0]
    ext = jnp.concatenate([xp_ref[0], x, xn_ref[0]], axis=0)
    hn = _rms_rows(ext) * (1.0 + sc_ref[0]) + sh_ref[0]
    hn = jnp.where(_halo_valid(tm), hn, 0.0).astype(BF16)
    acc = jnp.zeros_like(x)
    for c0 in range(0, dff, tn):
        ua = _dwconv_rows(_mm(hn, wu_ref[:, c0:c0 + tn]), cw_ref, c0, c0 + tn, FFN_CONV, tm)
        ug = _dwconv_rows(_mm(hn, wu_ref[:, dff + c0:dff + c0 + tn]), cw_ref, dff + c0, dff + c0 + tn,
                          FFN_CONV, tm)
        ua = ua + cb_ref[:, c0:c0 + tn]
        ug = ug + cb_ref[:, dff + c0:dff + c0 + tn]
        act = ua * (ug * _sigmoid(ug))
        acc = acc + _mm(act.astype(BF16), wd_ref[c0:c0 + tn, :])
    y = x + g_ref[0] * acc
    if final:
        y = _rms_rows(y) * fg_ref[...]
    o_ref[0] = y


def _conv_ffn(x, mod, w_up, conv_w, conv_b, w_down, final_g=None):
    b, s, d = x.shape
    dff = w_down.shape[0]
    tm = min(ROW_TILE, s)
    final = final_g is not None
    const = lambda i, j: (0, 0)
    in_specs = _seq_halo_specs(tm, s, d) + [
        pl.BlockSpec((1, 1, d), lambda i, j: (i, 0, 3)),
        pl.BlockSpec((1, 1, d), lambda i, j: (i, 0, 4)),
        pl.BlockSpec((1, 1, d), lambda i, j: (i, 0, 5)),
        pl.BlockSpec(w_up.shape, const),
        pl.BlockSpec(conv_w.shape, const),
        pl.BlockSpec((1, 2 * dff), const),
        pl.BlockSpec(w_down.shape, const)]
    args = [x, x, x, mod, mod, mod, w_up, conv_w, conv_b.reshape(1, 2 * dff), w_down]
    if final:
        in_specs.append(pl.BlockSpec((1, d), const))
        args.append(final_g.reshape(1, d))
    return pl.pallas_call(
        functools.partial(_ffn_kernel, tm=tm, dff=dff, tn=FFN_COLS, final=final),
        out_shape=jax.ShapeDtypeStruct((b, s, d), F32),
        grid=(b, s // tm),
        in_specs=in_specs,
        out_specs=pl.BlockSpec((1, tm, d), lambda i, j: (i, j, 0)),
        compiler_params=_params("parallel", "parallel"),
        name="conv_ffn",
    )(*args)


def _mlstm_layer(x, mod, w_in, gate_b, norm_g, w_out):
    heads = A_HEADS
    n_main = w_in.shape[1] - 4 * heads
    proj, gates = _inproj(x, mod, _split_w_in(w_in, n_main), n_main)
    hs = []
    for direction in range(2):
        lo = 2 * heads * direction
        gcol, grow = _gate_layouts(gates, lo, 2 * heads, min(MLSTM_CHUNK, x.shape[1]))
        bias = gate_b[lo:lo + 2 * heads].astype(F32)
        hs.append(_mlstm_scan(proj, gcol, grow, bias.reshape(1, -1), bias.reshape(-1, 1),
                              reverse=direction == 1))
    d = x.shape[2]
    return _outproj("mlstm", x, mod, w_out.astype(BF16), hs[0], hs[1], gate_src=proj,
                    gate_block=n_main // d - 1, norm_g=norm_g, hd=d // heads)


def _gdn_layer(x, mod, w_in, conv_w, a_log, dt_bias, norm_g, w_out):
    heads = B_HEADS
    n_main = w_in.shape[1] - 4 * heads
    qkv, z, gates = _gdn_inproj(x, mod, _split_w_in(w_in, n_main), conv_w)
    outs = []
    for direction in range(2):
        lo = 2 * heads * direction
        gcol = gates[:, :, lo:lo + 2 * heads]
        grow = jnp.swapaxes(gcol, 1, 2)
        p = jnp.stack([a_log[direction], dt_bias[direction]]).astype(F32)
        outs.append(_gdn_scan(qkv, gcol, grow, p, p.T, reverse=direction == 1))
    d = x.shape[2]
    return _outproj("gdn", x, mod, w_out.astype(BF16), outs[0], outs[1], gate_src=z,
                    gate_block=0, norm_g=jnp.tile(norm_g, heads), hd=d // heads)


def _attn_layer(x, mod, w_in, lam_p, norm_g, w_out, bias, lambda_init):
    n = w_in.shape[1]
    dh = n // (6 * C_HEADS)
    proj = _inproj(x, mod, w_in.astype(BF16), n, out_dtype=BF16, lead_cols=n // 3,
                   lead_scale=dh ** -0.5 * LOG2E)[0]
    o = _diff_attn(proj, bias, lam_p.astype(F32), norm_g, lambda_init)
    return _outproj("attn", x, mod, w_out.astype(BF16), o)


def _trunk(x, mods, bias_tiles, ada_w, ada_b, a_w_in, a_gate_b, a_norm_g, a_w_out, b_w_in, b_conv_w,
           b_a_log, b_dt_bias, b_norm_g, b_w_out, c_w_in, c_lambda, c_norm_g, c_w_out, rel_bias,
           ffn_w_up, ffn_conv_w, ffn_conv_b, ffn_w_down, final_g):
    depth = ada_w.shape[0]
    for i in range(depth):
        mod = mods[i][:, None, :]
        j = i // N_MIXERS
        kind = i % N_MIXERS
        if kind == 0:
            x = _mlstm_layer(x, mod, a_w_in[j], a_gate_b[j], a_norm_g[j], a_w_out[j])
        elif kind == 1:
            x = _gdn_layer(x, mod, b_w_in[j], b_conv_w[j], b_a_log[j], b_dt_bias[j], b_norm_g[j],
                           b_w_out[j])
        else:
            x = _attn_layer(x, mod, c_w_in[j], c_lambda[j], c_norm_g[j], c_w_out[j], bias_tiles,
                            0.8 - 0.6 * math.exp(-0.3 * i))
        x = _conv_ffn(x, mod, ffn_w_up[i].astype(BF16), ffn_conv_w[i], ffn_conv_b[i],
                      ffn_w_down[i].astype(BF16), final_g if i == depth - 1 else None)
    return x


def kernel(x_prompt, x_sample, c_prompt, c_sample, ada_w, ada_b, a_w_in, a_gate_b, a_norm_g, a_w_out, b_w_in, b_conv_w, b_a_log, b_dt_bias, b_norm_g, b_w_out, c_w_in, c_lambda, c_norm_g, c_w_out, rel_bias, ffn_w_up, ffn_conv_w, ffn_conv_b, ffn_w_down, final_g):
    weights = (ada_w, ada_b, a_w_in, a_gate_b, a_norm_g, a_w_out, b_w_in, b_conv_w, b_a_log,
               b_dt_bias, b_norm_g, b_w_out, c_w_in, c_lambda, c_norm_g, c_w_out, rel_bias,
               ffn_w_up, ffn_conv_w, ffn_conv_b, ffn_w_down, final_g)
    nb_p, nb_s = c_prompt.shape[0], c_sample.shape[0]
    c_all = jnp.concatenate([c_prompt, c_sample], axis=0)
    pad = (-c_all.shape[0]) % SUBLANES
    mods = _ada_mod(jnp.pad(c_all, ((0, pad), (0, 0))), ada_w, ada_b)
    outs, bias_tiles = [], {}
    for x, lo, n in ((x_prompt, 0, nb_p), (x_sample, nb_p, nb_s)):
        t = min(ATTN_TILE, x.shape[1])
        if t not in bias_tiles:
            bias_tiles[t] = _bias_tiles(rel_bias, t)
        outs.append(_trunk(x, mods[:, lo:lo + n], bias_tiles[t], *weights))
    return tuple(outs)
```

```python
import functools
import math

import jax
import jax.numpy as jnp
from jax import lax
from jax.experimental import pallas as pl
from jax.experimental.pallas import tpu as pltpu

F32 = jnp.float32
BF16 = jnp.bfloat16
EPS = 1e-6
LOG2E = math.log2(math.e)

N_MIXERS = 3
A_HEADS = 4
B_HEADS = 8
B_CONV = 5
C_HEADS = 8
REL_BUCKETS = 32
REL_MAX_DIST = 128
FFN_CONV = 3

LANES = 128
SUBLANES = 8
HALO = SUBLANES
VMEM_LIMIT = 56 * 1024 * 1024

ROW_TILE = 512
SCAN_BLOCK = 512
SCAN_CHUNK = 64
GDN_GROUP = 256
GDN_UNITS = 4
MLSTM_CHUNK = 256
ATTN_TILE = 512
ATTN_ROWS = 2048
ATTN_KEYS = 2048
MXU_WIDTH = 256
FFN_COLS = 6 * MXU_WIDTH


def _params(*sem):
    return pltpu.CompilerParams(dimension_semantics=sem, vmem_limit_bytes=VMEM_LIMIT)


def _sigmoid(x):
    return 1.0 / (1.0 + jnp.exp(-x))


def _softplus(x):
    return jnp.maximum(x, 0.0) + jnp.log(1.0 + jnp.exp(-jnp.abs(x)))


def _rms_rows(x):
    return x * lax.rsqrt(jnp.mean(x * x, axis=-1, keepdims=True) + EPS)


def _mm(a, b):
    return jnp.dot(a, b, preferred_element_type=F32)


def _mm_inv(a, b):
    return _mm(a.astype(BF16), b.astype(BF16))


def _mm_nt(a, b):
    return lax.dot_general(a, b, (((1,), (1,)), ((), ())), preferred_element_type=F32)


def _mm_tn(a, b):
    return lax.dot_general(a, b, (((0,), (0,)), ((), ())), preferred_element_type=F32)


def _mod_kernel(c_ref, w_ref, b_ref, o_ref):
    c = c_ref[...]
    cs = c * _sigmoid(c)
    o_ref[0] = _mm(cs.astype(BF16), w_ref[0].astype(BF16)) + b_ref[0]


def _ada_mod(c, ada_w, ada_b):
    depth, d, n = ada_w.shape
    bp = c.shape[0]
    tn = 1536
    return pl.pallas_call(
        _mod_kernel,
        out_shape=jax.ShapeDtypeStruct((depth, bp, n), F32),
        grid=(depth, n // tn),
        in_specs=[pl.BlockSpec((bp, d), lambda l, j: (0, 0)),
                  pl.BlockSpec((1, d, tn), lambda l, j: (l, 0, j)),
                  pl.BlockSpec((1, 1, tn), lambda l, j: (l, 0, j))],
        out_specs=pl.BlockSpec((1, bp, tn), lambda l, j: (l, 0, j)),
        compiler_params=_params("parallel", "parallel"),
        name="ada_mod",
    )(c, ada_w, ada_b.reshape(depth, 1, n))


def _inproj_kernel(x_ref, sh_ref, sc_ref, w_ref, *o_refs, n_main, tn, lead_cols, lead_scale):
    hn = (_rms_rows(x_ref[0]) * (1.0 + sc_ref[0]) + sh_ref[0]).astype(BF16)
    for c0 in range(0, n_main, tn):
        y = _mm(hn, w_ref[:, c0:c0 + tn])
        if c0 < lead_cols:
            y = y * lead_scale
        o_refs[0][0, :, c0:c0 + tn] = y.astype(o_refs[0].dtype)
    if len(o_refs) > 1:
        o_refs[1][0] = _mm(hn, w_ref[:, n_main:])


def _inproj(x, mod, w, n_main, out_dtype=F32, lead_cols=0, lead_scale=1.0):
    b, s, d = x.shape
    n_all = w.shape[1]
    tm = min(ROW_TILE, s)
    tn = 512
    assert lead_cols % tn == 0
    out_shape = [jax.ShapeDtypeStruct((b, s, n_main), out_dtype)]
    out_specs = [pl.BlockSpec((1, tm, n_main), lambda i, j: (i, j, 0))]
    if n_all > n_main:
        out_shape.append(jax.ShapeDtypeStruct((b, s, n_all - n_main), F32))
        out_specs.append(pl.BlockSpec((1, tm, n_all - n_main), lambda i, j: (i, j, 0)))
    return pl.pallas_call(
        functools.partial(_inproj_kernel, n_main=n_main, tn=tn, lead_cols=lead_cols,
                          lead_scale=lead_scale),
        out_shape=out_shape,
        grid=(b, s // tm),
        in_specs=[pl.BlockSpec((1, tm, d), lambda i, j: (i, j, 0)),
                  pl.BlockSpec((1, 1, d), lambda i, j: (i, 0, 0)),
                  pl.BlockSpec((1, 1, d), lambda i, j: (i, 0, 1)),
                  pl.BlockSpec((d, n_all), lambda i, j: (0, 0))],
        out_specs=out_specs,
        compiler_params=_params("parallel", "parallel"),
        name="inproj",
    )(x, mod, mod, w)


def _split_w_in(w, n_main):
    d, n = w.shape
    if n == n_main:
        return w.astype(BF16)
    pad = LANES - (n - n_main)
    return jnp.pad(w, ((0, 0), (0, pad))).astype(BF16)


def _gate_layouts(g, lo, width, chunk):
    b, s, _ = g.shape
    col = g[:, :, lo:lo + width].reshape(b, s // chunk, chunk, width)
    return col, jnp.swapaxes(col, 2, 3)


def _mlstm_kernel(q_ref, k_ref, v_ref, gcol_ref, grow_ref, bcol_ref, brow_ref, h_ref,
                  c_s, n_s, m_s, *, reverse, chunk, heads, dqk, dv):
    @pl.when(pl.program_id(1) == 0)
    def _():
        c_s[...] = jnp.zeros_like(c_s)
        n_s[...] = jnp.zeros_like(n_s)
        m_s[...] = jnp.zeros_like(m_s)

    ncb = gcol_ref.shape[1]
    row = lax.broadcasted_iota(jnp.int32, (chunk, chunk), 0)
    col = lax.broadcasted_iota(jnp.int32, (chunk, chunk), 1)
    mask = (col >= row) if reverse else (col <= row)
    mask_t = (row >= col) if reverse else (row <= col)
    scale = dqk ** -0.5

    def step(ci, carry):
        cc = (ncb - 1 - ci) if reverse else ci
        r0 = pl.multiple_of(cc * chunk, chunk)
        gcol = gcol_ref[0, cc] + bcol_ref[...]
        grow = grow_ref[0, cc] + brow_ref[...]
        fcol = -_softplus(-gcol)
        frow = -_softplus(-grow)
        hs = range(heads)
        i_col = [gcol[:, h:h + 1] for h in hs]
        f_col = [fcol[:, heads + h:heads + h + 1] for h in hs]
        i_row = [grow[h:h + 1, :] for h in hs]
        f_row = [frow[heads + h:heads + h + 1, :] for h in hs]
        b_col = [jnp.sum(jnp.where(mask, f_row[h], 0.0), axis=1, keepdims=True) for h in hs]
        b_row = [jnp.sum(jnp.where(mask_t, f_col[h], 0.0), axis=0, keepdims=True) for h in hs]
        total = [jnp.sum(f_row[h], axis=1, keepdims=True) for h in hs]
        m_prev = [m_s[h] for h in hs]
        q = [q_ref[0, pl.ds(r0, chunk), h * dqk:(h + 1) * dqk] * scale for h in hs]
        k = [k_ref[0, pl.ds(r0, chunk), h * dqk:(h + 1) * dqk] for h in hs]
        vb = [v_ref[0, pl.ds(r0, chunk), h * dv:(h + 1) * dv].astype(BF16) for h in hs]
        c_prev = [c_s[h] for h in hs]
        n_prev = [n_s[h] for h in hs]
        qkc = [_mm(q[h].astype(BF16),
                   jnp.concatenate([k[h].T.astype(BF16), c_prev[h].astype(BF16)], axis=1))
               for h in hs]
        dmat = [jnp.where(mask, b_col[h] + (i_row[h] - b_row[h]), -jnp.inf) for h in hs]
        inter = [b_col[h] + m_prev[h] for h in hs]
        m_row = [jnp.maximum(inter[h], jnp.max(dmat[h], axis=1, keepdims=True)) for h in hs]
        w_inter = [jnp.exp(inter[h] - m_row[h]) for h in hs]
        sqk = [qkc[h][:, :chunk] * jnp.exp(dmat[h] - m_row[h]) for h in hs]
        dlast = [total[h] + (i_col[h] - b_col[h]) for h in hs]
        m_new = [jnp.maximum(total[h] + m_prev[h], jnp.max(dlast[h], axis=0, keepdims=True))
                 for h in hs]
        kw = [k[h] * jnp.exp(dlast[h] - m_new[h]) for h in hs]
        both = [_mm(jnp.concatenate([sqk[h].astype(BF16), kw[h].T.astype(BF16)], axis=0), vb[h])
                for h in hs]
        for h in hs:
            num = w_inter[h] * qkc[h][:, chunk:] + both[h][:chunk]
            den = (w_inter[h] * jnp.sum(q[h] * n_prev[h], axis=1, keepdims=True)
                   + jnp.sum(sqk[h], axis=1, keepdims=True))
            h_ref[0, pl.ds(r0, chunk), h * dv:(h + 1) * dv] = (
                num / jnp.maximum(jnp.abs(den), jnp.exp(-m_row[h])))
            decay = jnp.exp(total[h] + m_prev[h] - m_new[h])
            c_s[h] = decay * c_prev[h] + both[h][chunk:]
            n_s[h] = decay * n_prev[h] + jnp.sum(kw[h], axis=0, keepdims=True)
            m_s[h] = m_new[h]
        return carry

    lax.fori_loop(0, ncb, step, 0)


def _mlstm_scan(proj, gcol, grow, bcol, brow, *, reverse):
    b, s, _ = proj.shape
    heads = A_HEADS
    dqk = proj.shape[2] // (6 * heads)
    dv = 2 * dqk
    nq, nv = heads * dqk, heads * dv
    blk = min(SCAN_BLOCK, s)
    nb = s // blk
    chunk = gcol.shape[2]
    ncb = blk // chunk
    ng = gcol.shape[3]

    def seq(j):
        return (nb - 1 - j) if reverse else j

    return pl.pallas_call(
        functools.partial(_mlstm_kernel, reverse=reverse, chunk=chunk, heads=heads,
                          dqk=dqk, dv=dv),
        out_shape=jax.ShapeDtypeStruct((b, s, nv), F32),
        grid=(b, nb),
        in_specs=[pl.BlockSpec((1, blk, nq), lambda i, j: (i, seq(j), 0)),
                  pl.BlockSpec((1, blk, nq), lambda i, j: (i, seq(j), 1)),
                  pl.BlockSpec((1, blk, nv), lambda i, j: (i, seq(j), 1)),
                  pl.BlockSpec((1, ncb, chunk, ng), lambda i, j: (i, seq(j), 0, 0)),
                  pl.BlockSpec((1, ncb, ng, chunk), lambda i, j: (i, seq(j), 0, 0)),
                  pl.BlockSpec((1, ng), lambda i, j: (0, 0)),
                  pl.BlockSpec((ng, 1), lambda i, j: (0, 0))],
        out_specs=pl.BlockSpec((1, blk, nv), lambda i, j: (i, seq(j), 0)),
        scratch_shapes=[pltpu.VMEM((heads, dqk, dv), F32),
                        pltpu.VMEM((heads, 1, dqk), F32),
                        pltpu.VMEM((heads, 1, 1), F32)],
        compiler_params=_params("parallel", "arbitrary"),
        name="mlstm_scan_bwd" if reverse else "mlstm_scan_fwd",
    )(proj, proj, proj, gcol, grow, bcol, brow)


def _seq_halo_specs(tm, s, width, col_block=0):
    r = tm // HALO
    last = s // HALO - 1

    def prev(i, j):
        return (i, jnp.maximum(j * r - 1, 0), col_block)

    def nxt(i, j):
        return (i, jnp.minimum((j + 1) * r, last), col_block)

    return [pl.BlockSpec((1, HALO, width), prev),
            pl.BlockSpec((1, tm, width), lambda i, j: (i, j, col_block)),
            pl.BlockSpec((1, HALO, width), nxt)]


def _halo_valid(tm):
    j = pl.program_id(1)
    rid = lax.broadcasted_iota(jnp.int32, (tm + 2 * HALO, 1), 0)
    return jnp.logical_and(jnp.logical_or(rid >= HALO, j > 0),
                           jnp.logical_or(rid < tm + HALO, j < pl.num_programs(1) - 1))


def _dwconv_rows(ext, w_ref, c0, c1, taps, tm):
    n = ext.shape[0]
    acc = None
    for t in range(taps):
        off = t - taps // 2
        src = ext if off == 0 else pltpu.roll(ext, (-off) % n, 0)
        term = src * w_ref[t:t + 1, c0:c1]
        acc = term if acc is None else acc + term
    return acc[HALO:HALO + tm]


def _gdn_inproj_kernel(xp_ref, x_ref, xn_ref, sh_ref, sc_ref, w_ref, cw_ref, qkv_ref, z_ref, g_ref,
                       *, tm, heads, dk, tn):
    ext = jnp.concatenate([xp_ref[0], x_ref[0], xn_ref[0]], axis=0)
    hn = _rms_rows(ext) * (1.0 + sc_ref[0]) + sh_ref[0]
    hn = jnp.where(_halo_valid(tm), hn, 0.0).astype(BF16)
    hw = heads * dk
    for c0 in range(0, 3 * hw, tn):
        y = _dwconv_rows(_mm(hn, w_ref[:, c0:c0 + tn]), cw_ref, c0, c0 + tn, B_CONV, tm)
        y = y * _sigmoid(y)
        for g0 in range(0, tn, dk):
            yh = y[:, g0:g0 + dk]
            if c0 < 2 * hw:
                yh = yh * lax.rsqrt(jnp.sum(yh * yh, axis=-1, keepdims=True) + EPS)
            if c0 < hw:
                yh = yh * (dk ** -0.5)
            qkv_ref[0, :, c0 + g0:c0 + g0 + dk] = yh
    hm = hn[HALO:HALO + tm]
    for c0 in range(0, hw, tn):
        z_ref[0, :, c0:c0 + tn] = _mm(hm, w_ref[:, 3 * hw + c0:3 * hw + c0 + tn])
    g_ref[0] = _mm(hm, w_ref[:, 4 * hw:])


def _gdn_inproj(x, mod, w, conv_w):
    b, s, d = x.shape
    heads = B_HEADS
    n_gate = LANES
    hw = (w.shape[1] - n_gate) // 4
    tm = min(ROW_TILE, s)
    const = lambda i, j: (0, 0)
    row = lambda n: pl.BlockSpec((1, tm, n), lambda i, j: (i, j, 0))
    return pl.pallas_call(
        functools.partial(_gdn_inproj_kernel, tm=tm, heads=heads, dk=hw // heads, tn=512),
        out_shape=[jax.ShapeDtypeStruct((b, s, 3 * hw), F32),
                   jax.ShapeDtypeStruct((b, s, hw), F32),
                   jax.ShapeDtypeStruct((b, s, n_gate), F32)],
        grid=(b, s // tm),
        in_specs=_seq_halo_specs(tm, s, d) + [
            pl.BlockSpec((1, 1, d), lambda i, j: (i, 0, 0)),
            pl.BlockSpec((1, 1, d), lambda i, j: (i, 0, 1)),
            pl.BlockSpec(w.shape, const),
            pl.BlockSpec(conv_w.shape, const)],
        out_specs=[row(3 * hw), row(hw), row(n_gate)],
        compiler_params=_params("parallel", "parallel"),
        name="gdn_inproj",
    )(x, x, x, mod, mod, w, conv_w)


def _gdn_kernel(q_ref, k_ref, v_ref, gcol_ref, grow_ref, pcol_ref, prow_ref, o_ref,
                s_s, mq_s, r_s, gl_s, *, reverse, chunk, group, heads, dk):
    @pl.when(pl.program_id(1) == 0)
    def _():
        s_s[...] = jnp.zeros_like(s_s)

    blk = q_ref.shape[1]
    cpg = group // chunk
    clog = chunk.bit_length() - 1
    row = lax.broadcasted_iota(jnp.int32, (group, group), 0)
    col = lax.broadcasted_iota(jnp.int32, (group, group), 1)

    def same_block(log2):
        return jnp.right_shift(row, log2) == jnp.right_shift(col, log2)

    same_chunk = same_block(clog)
    causal = jnp.logical_and(same_chunk, (col >= row) if reverse else (col <= row))
    causal_t = jnp.logical_and(same_chunk, (row >= col) if reverse else (row <= col))
    strict = jnp.logical_and(same_chunk, (col > row) if reverse else (col < row))
    eye = (row == col).astype(F32)
    base_log2 = 3
    causal_f, causal_tf = causal.astype(F32), causal_t.astype(F32)
    same_chunk_f, strict_f = same_chunk.astype(F32), strict.astype(F32)
    same_base_b = same_block(base_log2).astype(BF16)
    level_masks_b = [jnp.logical_and(same_block(lg + 1),
                                     jnp.logical_not(same_block(lg))).astype(BF16)
                     for lg in range(base_log2, clog)]
    chunk_of_row = jnp.right_shift(lax.broadcasted_iota(jnp.int32, (group, 1), 0), clog)

    raw_col = gcol_ref[0]
    raw_row = grow_ref[0]
    g_cols = -jnp.exp(pcol_ref[0:1, :]) * _softplus(raw_col[:, :heads] + pcol_ref[1:2, :])
    g_rows = -jnp.exp(prow_ref[:, 0:1]) * _softplus(raw_row[:heads, :] + prow_ref[:, 1:2])
    beta_cols = _sigmoid(raw_col[:, heads:])

    def par(fn, *lists):
        return [fn(*args) for args in zip(*lists)]

    def chunk_local(units):
        rows = [slice(gi * group, (gi + 1) * group) for _, gi in units]
        cols = [slice(h * dk, (h + 1) * dk) for h, _ in units]
        g_col = [g_cols[r, h:h + 1] for (h, _), r in zip(units, rows)]
        g_row = [g_rows[h:h + 1, r] for (h, _), r in zip(units, rows)]
        beta = [beta_cols[r, h:h + 1] for (h, _), r in zip(units, rows)]
        gc_col = par(lambda g: jnp.sum(causal_f * g, axis=1, keepdims=True), g_row)
        gc_row = par(lambda g: jnp.sum(causal_tf * g, axis=0, keepdims=True), g_col)
        tot_col = par(lambda g: jnp.sum(same_chunk_f * g, axis=1, keepdims=True), g_row)
        decay = par(lambda c, r: jnp.exp(jnp.where(causal, c - r, -jnp.inf)), gc_col, gc_row)
        q = [q_ref[0, r, c] for r, c in zip(rows, cols)]
        k = [k_ref[0, r, c] for r, c in zip(rows, cols)]
        v = [v_ref[0, r, c] for r, c in zip(rows, cols)]
        kb = par(lambda x: x.astype(BF16), k)
        kq = par(lambda kb_, q_: _mm_nt(jnp.concatenate([kb_, q_.astype(BF16)], axis=0), kb_),
                 kb, q)
        a = par(lambda kq_, d, b_: (kq_[:group] * d * strict_f * b_).astype(BF16), kq, decay, beta)
        n1 = par(lambda a_: a_ * same_base_b, a)
        n2 = par(lambda n: _mm_inv(n, n), n1)
        n4 = par(lambda n: _mm_inv(n, n), n2)
        x = par(lambda n, m: _mm_inv(eye - n, eye + m), n1, n2)
        x = par(lambda x_, n: _mm_inv(x_, eye + n), x, n4)
        for lm in level_masks_b:
            xl = par(lambda x_, a_: _mm_inv(x_, a_ * lm), x, a)
            x = par(lambda x_, xl_: x_ - _mm_inv(xl_, x_), x, xl)
        eg = par(jnp.exp, gc_col)
        wu = par(lambda x_, k_, b_, e, v_: _mm_inv(
            x_, jnp.concatenate([k_ * b_ * e, v_ * b_], axis=1)).astype(BF16), x, k, beta, eg, v)
        qo = par(lambda kq_, d, wu_: _mm((kq_[group:] * d).astype(BF16), wu_), kq, decay, wu)
        kd_by_chunk = par(lambda k_, t_, c_: jnp.concatenate(
            [jnp.where(chunk_of_row == c, k_ * jnp.exp(t_ - c_), 0.0) for c in range(cpg)],
            axis=1).astype(BF16), k, tot_col, gc_col)
        mr = par(_mm_tn, kd_by_chunk, wu)
        for i, (h, gi) in enumerate(units):
            o_ref[0, rows[i], cols[i]] = qo[i][:, dk:]
            q_prime = (q[i] * eg[i] - qo[i][:, :dk]).astype(BF16)
            for c in range(cpg):
                cg = gi * cpg + c
                mq_s[h, cg, 0:dk, :] = mr[i][c * dk:(c + 1) * dk, :dk].astype(BF16)
                mq_s[h, cg, dk:dk + chunk, :] = q_prime[c * chunk:(c + 1) * chunk]
                r_s[h, cg] = mr[i][c * dk:(c + 1) * dk, dk:]
                total = jnp.sum(g_row[i][:, c * chunk:(c + 1) * chunk], axis=1, keepdims=True)
                gl_s[h, cg] = jnp.broadcast_to(jnp.exp(total), (1, dk))

    all_units = [(h, gi) for h in range(heads) for gi in range(blk // group)]
    for u0 in range(0, len(all_units), GDN_UNITS):
        chunk_local(all_units[u0:u0 + GDN_UNITS])

    ncb = blk // chunk
    for ci in range(ncb):
        cc = (ncb - 1 - ci) if reverse else ci
        s_prev = [s_s[h] for h in range(heads)]
        p = [_mm(mq_s[h, cc], s_prev[h].astype(BF16)) for h in range(heads)]
        for h in range(heads):
            s_s[h] = gl_s[h, cc] * s_prev[h] - p[h][:dk] + r_s[h, cc]
            o_ref[0, cc * chunk:(cc + 1) * chunk, h * dk:(h + 1) * dk] += p[h][dk:]


def _gdn_scan(qkv, gcol, grow, pcol, prow, *, reverse):
    b, s, n = qkv.shape
    heads = B_HEADS
    hw = n // 3
    dk = hw // heads
    blk = min(SCAN_BLOCK, s)
    nb = s // blk
    ncb = blk // SCAN_CHUNK
    ng = gcol.shape[2]

    def seq(j):
        return (nb - 1 - j) if reverse else j

    return pl.pallas_call(
        functools.partial(_gdn_kernel, reverse=reverse, chunk=SCAN_CHUNK, group=GDN_GROUP,
                          heads=heads, dk=dk),
        out_shape=jax.ShapeDtypeStruct((b, s, hw), F32),
        grid=(b, nb),
        in_specs=[pl.BlockSpec((1, blk, hw), lambda i, j: (i, seq(j), 0)),
                  pl.BlockSpec((1, blk, hw), lambda i, j: (i, seq(j), 1)),
                  pl.BlockSpec((1, blk, hw), lambda i, j: (i, seq(j), 2)),
                  pl.BlockSpec((1, blk, ng), lambda i, j: (i, seq(j), 0)),
                  pl.BlockSpec((1, ng, blk), lambda i, j: (i, 0, seq(j))),
                  pl.BlockSpec((2, heads), lambda i, j: (0, 0)),
                  pl.BlockSpec((heads, 2), lambda i, j: (0, 0))],
        out_specs=pl.BlockSpec((1, blk, hw), lambda i, j: (i, seq(j), 0)),
        scratch_shapes=[pltpu.VMEM((heads, dk, dk), F32),
                        pltpu.VMEM((heads, ncb, dk + SCAN_CHUNK, dk), BF16),
                        pltpu.VMEM((heads, ncb, dk, dk), F32),
                        pltpu.VMEM((heads, ncb, 1, dk), F32)],
        compiler_params=_params("parallel", "arbitrary"),
        name="gdn_scan_bwd" if reverse else "gdn_scan_fwd",
    )(qkv, qkv, qkv, gcol, grow, pcol, prow)


def _rel_bucket(rel):
    nb = REL_BUCKETS // 2
    exact = nb // 2
    n = jnp.abs(rel)
    large = exact + (jnp.log(jnp.maximum(n, 1).astype(jnp.float32) / exact)
                     / math.log(REL_MAX_DIST / exact) * (nb - exact)).astype(jnp.int32)
    large = jnp.minimum(large, nb - 1)
    return jnp.where(rel > 0, nb, 0) + jnp.where(n < exact, n, large)


def _bias_tiles(rel_bias, t):
    assert t >= REL_MAX_DIST
    n = 2 * t + 1
    m = jnp.arange(n, dtype=jnp.int32)
    rel_in_tile = jnp.where(m < t, m, m - n)
    d = jnp.arange(-2, 3, dtype=jnp.int32)
    rel = d[:, None] * t + rel_in_tile[None, :]
    vals = jnp.transpose(rel_bias.astype(F32)[_rel_bucket(rel)], (2, 0, 1)) * LOG2E
    h = vals.shape[0]
    tiled = jnp.tile(vals, (1, 1, t))[:, :, :t * (n - 1)].reshape(h, 5, t, n - 1)
    return tiled[:, :, :, :t]


def _attn_kernel(q_ref, k_ref, v_ref, bias_ref, lam_ref, ng_ref, o_ref,
                 m1_s, a1_s, m2_s, a2_s, vext_s, *, lambda_init, dh):
    qi = pl.program_id(2)
    ki = pl.program_id(3)
    t = bias_ref.shape[2]
    nq = q_ref.shape[1] // t
    tk, dv = v_ref.shape[1], v_ref.shape[2]
    r = tk // t
    maps = ((m1_s, a1_s), (m2_s, a2_s))

    @pl.when(ki == 0)
    def _():
        for m_s, a_s in maps:
            m_s[...] = jnp.full_like(m_s, -jnp.inf)
            a_s[...] = jnp.zeros_like(a_s)
        vext_s[:, dv:] = jnp.ones((tk, dv), BF16)

    vext_s[:, :dv] = v_ref[0]
    k_maps = [k_ref[0, :, i * dh:(i + 1) * dh] for i in range(2)]
    first = ki * r - qi * nq

    def update(with_bias):
        def logits(iq):
            rows = slice(iq * t, (iq + 1) * t)
            s = [_mm_nt(q_ref[0, rows, i * dh:(i + 1) * dh], k_maps[i]) for i in range(2)]
            if with_bias:
                bias = jnp.concatenate(
                    [bias_ref[0, jnp.clip(first - iq + j, -2, 2) + 2] for j in range(r)], axis=1)
                s = [s_ + bias for s_ in s]
            return s

        s_next = logits(0)
        for iq in range(nq):
            rows = slice(iq * t, (iq + 1) * t)
            s = s_next
            if iq + 1 < nq:
                s_next = logits(iq + 1)
            shift = 0.0 if with_bias else bias_ref[0, jnp.clip(first - iq, -2, 2) + 2, 0:1, 0:1]
            m_prev = [m_s[rows] for m_s, _ in maps]
            m_new = [jnp.maximum(m_prev[i], jnp.max(s[i], axis=1, keepdims=True) + shift)
                     for i in range(2)]
            p = [jnp.exp2(s[i] - (m_new[i] - shift)).astype(BF16) for i in range(2)]
            pv = [_mm(p[i], vext_s[...]) for i in range(2)]
            for i, (m_s, a_s) in enumerate(maps):
                a_s[rows] = jnp.exp2(m_prev[i] - m_new[i]) * a_s[rows] + pv[i]
                m_s[rows] = m_new[i]

    far = jnp.logical_or(first - (nq - 1) >= 2, first + (r - 1) <= -2)

    @pl.when(far)
    def _():
        update(False)

    @pl.when(jnp.logical_not(far))
    def _():
        update(True)

    @pl.when(ki == pl.num_programs(3) - 1)
    def _():
        lp = lam_ref[...]
        lam = (jnp.exp(jnp.sum(lp[0:1] * lp[1:2], axis=1, keepdims=True))
               - jnp.exp(jnp.sum(lp[2:3] * lp[3:4], axis=1, keepdims=True)) + lambda_init)
        a1, a2 = a1_s[...], a2_s[...]
        o = a1[:, :dv] / a1[:, dv:dv + 1] - lam * (a2[:, :dv] / a2[:, dv:dv + 1])
        o_ref[0] = _rms_rows(o) * ng_ref[...] * (1.0 - lambda_init)


def _diff_attn(proj, bias, lam_p, norm_g, lambda_init):
    b, s, n = proj.shape
    heads = C_HEADS
    dv = n // (3 * heads)
    t = bias.shape[2]
    tk = min(ATTN_KEYS, s)
    tq = min(ATTN_ROWS, s)
    return pl.pallas_call(
        functools.partial(_attn_kernel, lambda_init=lambda_init, dh=dv // 2),
        out_shape=jax.ShapeDtypeStruct((b, s, heads * dv), F32),
        grid=(b, heads, s // tq, s // tk),
        in_specs=[pl.BlockSpec((1, tq, dv), lambda i, h, qi, ki: (i, qi, h)),
                  pl.BlockSpec((1, tk, dv), lambda i, h, qi, ki: (i, ki, heads + h)),
                  pl.BlockSpec((1, tk, dv), lambda i, h, qi, ki: (i, ki, 2 * heads + h)),
                  pl.BlockSpec((1, 5, t, t), lambda i, h, qi, ki: (h, 0, 0, 0)),
                  pl.BlockSpec(lam_p.shape, lambda i, h, qi, ki: (0, 0)),
                  pl.BlockSpec((1, dv), lambda i, h, qi, ki: (0, 0))],
        out_specs=pl.BlockSpec((1, tq, dv), lambda i, h, qi, ki: (i, qi, h)),
        scratch_shapes=[pltpu.VMEM((tq, 1), F32), pltpu.VMEM((tq, 2 * dv), F32),
                        pltpu.VMEM((tq, 1), F32), pltpu.VMEM((tq, 2 * dv), F32),
                        pltpu.VMEM((tk, 2 * dv), BF16)],
        compiler_params=_params("parallel", "parallel", "parallel", "arbitrary"),
        name="diff_attn",
    )(proj, proj, proj, bias, lam_p, norm_g.reshape(1, dv))


def _outproj_kernel(*refs, mode, hd):
    if mode == "attn":
        a_ref, w_ref, x_ref, g_ref, o_ref = refs
        act = a_ref[0]
    else:
        a_ref, b_ref, gate_ref, ng_ref, w_ref, x_ref, g_ref, o_ref = refs
        ssum = a_ref[0] + b_ref[0]
        d = ssum.shape[1]
        hs = jnp.concatenate([_rms_rows(ssum[:, c0:c0 + hd]) for c0 in range(0, d, hd)], axis=1)
        hs = hs * ng_ref[...]
        gate = gate_ref[0]
        if mode == "mlstm":
            act = _sigmoid(gate) * hs
        else:
            act = hs * (gate * _sigmoid(gate))
    o_ref[0] = x_ref[0] + g_ref[0] * _mm(act.astype(BF16), w_ref[...])


def _outproj(mode, x, mod, w, a, b=None, gate_src=None, gate_block=0, norm_g=None, hd=0):
    bsz, s, d = x.shape
    tm = min(ROW_TILE, s)
    row = pl.BlockSpec((1, tm, d), lambda i, j: (i, j, 0))
    in_specs, args = [row], [a]
    if mode != "attn":
        in_specs += [row, pl.BlockSpec((1, tm, d), lambda i, j: (i, j, gate_block)),
                     pl.BlockSpec((1, d), lambda i, j: (0, 0))]
        args += [b, gate_src, norm_g.reshape(1, d)]
    in_specs += [pl.BlockSpec((d, d), lambda i, j: (0, 0)), row,
                 pl.BlockSpec((1, 1, d), lambda i, j: (i, 0, 2))]
    args += [w, x, mod]
    return pl.pallas_call(
        functools.partial(_outproj_kernel, mode=mode, hd=hd),
        out_shape=jax.ShapeDtypeStruct((bsz, s, d), F32),
        grid=(bsz, s // tm),
        in_specs=in_specs,
        out_specs=row,
        compiler_params=_params("parallel", "parallel"),
        name="outproj_" + mode,
    )(*args)


def _ffn_kernel(*refs, tm, dff, tn, final):
    if final:
        xp_ref, x_ref, xn_ref, sh_ref, sc_ref, g_ref, wu_ref, cw_ref, cb_ref, wd_ref, fg_ref, o_ref = refs
    else:
        xp_ref, x_ref, xn_ref, sh_ref, sc_ref, g_ref, wu_ref, cw_ref, cb_ref, wd_ref, o_ref = refs
    x = x_ref[0]
    ext = jnp.concatenate([xp_ref[0], x, xn_ref[0]], axis=0)
    hn = _rms_rows(ext) * (1.0 + sc_ref[0]) + sh_ref[0]
    hn = jnp.where(_halo_valid(tm), hn, 0.0).astype(BF16)
    acc = jnp.zeros_like(x)
    for c0 in range(0, dff, tn):
        c1 = min(c0 + tn, dff)
        ua = _dwconv_rows(_mm(hn, wu_ref[:, c0:c1]), cw_ref, c0, c1, FFN_CONV, tm)
        ug = _dwconv_rows(_mm(hn, wu_ref[:, dff + c0:dff + c1]), cw_ref, dff + c0, dff + c1,
                          FFN_CONV, tm)
        ua = ua + cb_ref[:, c0:c1]
        ug = ug + cb_ref[:, dff + c0:dff + c1]
        act = ua * (ug * _sigmoid(ug))
        acc = acc + _mm(act.astype(BF16), wd_ref[c0:c1, :])
    y = x + g_ref[0] * acc
    if final:
        y = _rms_rows(y) * fg_ref[...]
    o_ref[0] = y


def _conv_ffn(x, mod, w_up, conv_w, conv_b, w_down, final_g=None):
    b, s, d = x.shape
    dff = w_down.shape[0]
    tm = min(ROW_TILE, s)
    final = final_g is not None
    const = lambda i, j: (0, 0)
    in_specs = _seq_halo_specs(tm, s, d) + [
        pl.BlockSpec((1, 1, d), lambda i, j: (i, 0, 3)),
        pl.BlockSpec((1, 1, d), lambda i, j: (i, 0, 4)),
        pl.BlockSpec((1, 1, d), lambda i, j: (i, 0, 5)),
        pl.BlockSpec(w_up.shape, const),
        pl.BlockSpec(conv_w.shape, const),
        pl.BlockSpec((1, 2 * dff), const),
        pl.BlockSpec(w_down.shape, const)]
    args = [x, x, x, mod, mod, mod, w_up, conv_w, conv_b.reshape(1, 2 * dff), w_down]
    if final:
        in_specs.append(pl.BlockSpec((1, d), const))
        args.append(final_g.reshape(1, d))
    return pl.pallas_call(
        functools.partial(_ffn_kernel, tm=tm, dff=dff, tn=FFN_COLS, final=final),
        out_shape=jax.ShapeDtypeStruct((b, s, d), F32),
        grid=(b, s // tm),
        in_specs=in_specs,
        out_specs=pl.BlockSpec((1, tm, d), lambda i, j: (i, j, 0)),
        compiler_params=_params("parallel", "parallel"),
        name="conv_ffn",
    )(*args)


def _mlstm_layer(x, mod, w_in, gate_b, norm_g, w_out):
    heads = A_HEADS
    n_main = w_in.shape[1] - 4 * heads
    proj, gates = _inproj(x, mod, _split_w_in(w_in, n_main), n_main)
    hs = []
    for direction in range(2):
        lo = 2 * heads * direction
        gcol, grow = _gate_layouts(gates, lo, 2 * heads, min(MLSTM_CHUNK, x.shape[1]))
        bias = gate_b[lo:lo + 2 * heads].astype(F32)
        hs.append(_mlstm_scan(proj, gcol, grow, bias.reshape(1, -1), bias.reshape(-1, 1),
                              reverse=direction == 1))
    d = x.shape[2]
    return _outproj("mlstm", x, mod, w_out.astype(BF16), hs[0], hs[1], gate_src=proj,
                    gate_block=n_main // d - 1, norm_g=norm_g, hd=d // heads)


def _gdn_layer(x, mod, w_in, conv_w, a_log, dt_bias, norm_g, w_out):
    heads = B_HEADS
    n_main = w_in.shape[1] - 4 * heads
    qkv, z, gates = _gdn_inproj(x, mod, _split_w_in(w_in, n_main), conv_w)
    outs = []
    for direction in range(2):
        lo = 2 * heads * direction
        gcol = gates[:, :, lo:lo + 2 * heads]
        grow = jnp.swapaxes(gcol, 1, 2)
        p = jnp.stack([a_log[direction], dt_bias[direction]]).astype(F32)
        outs.append(_gdn_scan(qkv, gcol, grow, p, p.T, reverse=direction == 1))
    d = x.shape[2]
    return _outproj("gdn", x, mod, w_out.astype(BF16), outs[0], outs[1], gate_src=z,
                    gate_block=0, norm_g=jnp.tile(norm_g, heads), hd=d // heads)


def _attn_layer(x, mod, w_in, lam_p, norm_g, w_out, bias, lambda_init):
    n = w_in.shape[1]
    dh = n // (6 * C_HEADS)
    proj = _inproj(x, mod, w_in.astype(BF16), n, out_dtype=BF16, lead_cols=n // 3,
                   lead_scale=dh ** -0.5 * LOG2E)[0]
    o = _diff_attn(proj, bias, lam_p.astype(F32), norm_g, lambda_init)
    return _outproj("attn", x, mod, w_out.astype(BF16), o)


def _trunk(x, mods, bias_tiles, ada_w, ada_b, a_w_in, a_gate_b, a_norm_g, a_w_out, b_w_in, b_conv_w,
           b_a_log, b_dt_bias, b_norm_g, b_w_out, c_w_in, c_lambda, c_norm_g, c_w_out, rel_bias,
           ffn_w_up, ffn_conv_w, ffn_conv_b, ffn_w_down, final_g):
    depth = ada_w.shape[0]
    for i in range(depth):
        mod = mods[i][:, None, :]
        j = i // N_MIXERS
        kind = i % N_MIXERS
        if kind == 0:
            x = _mlstm_layer(x, mod, a_w_in[j], a_gate_b[j], a_norm_g[j], a_w_out[j])
        elif kind == 1:
            x = _gdn_layer(x, mod, b_w_in[j], b_conv_w[j], b_a_log[j], b_dt_bias[j], b_norm_g[j],
                           b_w_out[j])
        else:
            x = _attn_layer(x, mod, c_w_in[j], c_lambda[j], c_norm_g[j], c_w_out[j], bias_tiles,
                            0.8 - 0.6 * math.exp(-0.3 * i))
        x = _conv_ffn(x, mod, ffn_w_up[i].astype(BF16), ffn_conv_w[i], ffn_conv_b[i],
                      ffn_w_down[i].astype(BF16), final_g if i == depth - 1 else None)
    return x


def kernel(x_prompt, x_sample, c_prompt, c_sample, ada_w, ada_b, a_w_in, a_gate_b, a_norm_g, a_w_out, b_w_in, b_conv_w, b_a_log, b_dt_bias, b_norm_g, b_w_out, c_w_in, c_lambda, c_norm_g, c_w_out, rel_bias, ffn_w_up, ffn_conv_w, ffn_conv_b, ffn_w_down, final_g):
    weights = (ada_w, ada_b, a_w_in, a_gate_b, a_norm_g, a_w_out, b_w_in, b_conv_w, b_a_log,
               b_dt_bias, b_norm_g, b_w_out, c_w_in, c_lambda, c_norm_g, c_w_out, rel_bias,
               ffn_w_up, ffn_conv_w, ffn_conv_b, ffn_w_down, final_g)
    nb_p, nb_s = c_prompt.shape[0], c_sample.shape[0]
    c_all = jnp.concatenate([c_prompt, c_sample], axis=0)
    pad = (-c_all.shape[0]) % SUBLANES
    mods = _ada_mod(jnp.pad(c_all, ((0, pad), (0, 0))), ada_w, ada_b)
    outs, bias_tiles = [], {}
    for x, lo, n in ((x_prompt, 0, nb_p), (x_sample, nb_p, nb_s)):
        t = min(ATTN_TILE, x.shape[1])
        if t not in bias_tiles:
            bias_tiles[t] = _bias_tiles(rel_bias, t)
        outs.append(_trunk(x, mods[:, lo:lo + n], bias_tiles[t], *weights))
    return tuple(outs)
```

```python
import functools
import math

import jax
import jax.numpy as jnp
from jax import lax
from jax.experimental import pallas as pl
from jax.experimental.pallas import tpu as pltpu

F32 = jnp.float32
BF16 = jnp.bfloat16
EPS = 1e-6
LOG2E = math.log2(math.e)

N_MIXERS = 3
A_HEADS = 4
B_HEADS = 8
B_CONV = 5
C_HEADS = 8
REL_BUCKETS = 32
REL_MAX_DIST = 128
FFN_CONV = 3

LANES = 128
SUBLANES = 8
HALO = SUBLANES
VMEM_LIMIT = 56 * 1024 * 1024

ROW_TILE = 512
SCAN_BLOCK = 512
SCAN_CHUNK = 64
GDN_GROUP = 256
GDN_UNITS = 4
MLSTM_CHUNK = 256
ATTN_TILE = 512
ATTN_ROWS = 2048
ATTN_KEYS = 2048
MXU_WIDTH = 256
FFN_COLS = 6 * MXU_WIDTH


def _params(*sem):
    return pltpu.CompilerParams(dimension_semantics=sem, vmem_limit_bytes=VMEM_LIMIT)


def _sigmoid(x):
    return 1.0 / (1.0 + jnp.exp(-x))


def _softplus(x):
    return jnp.maximum(x, 0.0) + jnp.log(1.0 + jnp.exp(-jnp.abs(x)))


def _rms_rows(x):
    return x * lax.rsqrt(jnp.mean(x * x, axis=-1, keepdims=True) + EPS)


def _mm(a, b):
    return jnp.dot(a, b, preferred_element_type=F32)


def _mm_inv(a, b):
    return _mm(a.astype(BF16), b.astype(BF16))


def _mm_nt(a, b):
    return lax.dot_general(a, b, (((1,), (1,)), ((), ())), preferred_element_type=F32)


def _mm_tn(a, b):
    return lax.dot_general(a, b, (((0,), (0,)), ((), ())), preferred_element_type=F32)


def _mod_kernel(c_ref, w_ref, b_ref, o_ref):
    c = c_ref[...]
    cs = c * _sigmoid(c)
    o_ref[0] = _mm(cs.astype(BF16), w_ref[0].astype(BF16)) + b_ref[0]


def _ada_mod(c, ada_w, ada_b):
    depth, d, n = ada_w.shape
    bp = c.shape[0]
    tn = 1536
    return pl.pallas_call(
        _mod_kernel,
        out_shape=jax.ShapeDtypeStruct((depth, bp, n), F32),
        grid=(depth, n // tn),
        in_specs=[pl.BlockSpec((bp, d), lambda l, j: (0, 0)),
                  pl.BlockSpec((1, d, tn), lambda l, j: (l, 0, j)),
                  pl.BlockSpec((1, 1, tn), lambda l, j: (l, 0, j))],
        out_specs=pl.BlockSpec((1, bp, tn), lambda l, j: (l, 0, j)),
        compiler_params=_params("parallel", "parallel"),
        name="ada_mod",
    )(c, ada_w, ada_b.reshape(depth, 1, n))


def _inproj_kernel(x_ref, sh_ref, sc_ref, w_ref, *o_refs, n_main, tn, lead_cols, lead_scale):
    hn = (_rms_rows(x_ref[0]) * (1.0 + sc_ref[0]) + sh_ref[0]).astype(BF16)
    for c0 in range(0, n_main, tn):
        y = _mm(hn, w_ref[:, c0:c0 + tn])
        if c0 < lead_cols:
            y = y * lead_scale
        o_refs[0][0, :, c0:c0 + tn] = y.astype(o_refs[0].dtype)
    if len(o_refs) > 1:
        o_refs[1][0] = _mm(hn, w_ref[:, n_main:])


def _inproj(x, mod, w, n_main, out_dtype=F32, lead_cols=0, lead_scale=1.0):
    b, s, d = x.shape
    n_all = w.shape[1]
    tm = min(ROW_TILE, s)
    tn = 512
    assert lead_cols % tn == 0
    out_shape = [jax.ShapeDtypeStruct((b, s, n_main), out_dtype)]
    out_specs = [pl.BlockSpec((1, tm, n_main), lambda i, j: (i, j, 0))]
    if n_all > n_main:
        out_shape.append(jax.ShapeDtypeStruct((b, s, n_all - n_main), F32))
        out_specs.append(pl.BlockSpec((1, tm, n_all - n_main), lambda i, j: (i, j, 0)))
    return pl.pallas_call(
        functools.partial(_inproj_kernel, n_main=n_main, tn=tn, lead_cols=lead_cols,
                          lead_scale=lead_scale),
        out_shape=out_shape,
        grid=(b, s // tm),
        in_specs=[pl.BlockSpec((1, tm, d), lambda i, j: (i, j, 0)),
                  pl.BlockSpec((1, 1, d), lambda i, j: (i, 0, 0)),
                  pl.BlockSpec((1, 1, d), lambda i, j: (i, 0, 1)),
                  pl.BlockSpec((d, n_all), lambda i, j: (0, 0))],
        out_specs=out_specs,
        compiler_params=_params("parallel", "parallel"),
        name="inproj",
    )(x, mod, mod, w)


def _split_w_in(w, n_main):
    d, n = w.shape
    if n == n_main:
        return w.astype(BF16)
    pad = LANES - (n - n_main)
    return jnp.pad(w, ((0, 0), (0, pad))).astype(BF16)


def _gate_layouts(g, lo, width, chunk):
    b, s, _ = g.shape
    col = g[:, :, lo:lo + width].reshape(b, s // chunk, chunk, width)
    return col, jnp.swapaxes(col, 2, 3)


def _mlstm_kernel(q_ref, k_ref, v_ref, gcol_ref, grow_ref, bcol_ref, brow_ref, *rest,
                  reverse, chunk, heads, dqk, dv):
    add_ref = rest[0] if len(rest) == 5 else None
    h_ref, c_s, n_s, m_s = rest[-4:]

    @pl.when(pl.program_id(1) == 0)
    def _():
        c_s[...] = jnp.zeros_like(c_s)
        n_s[...] = jnp.zeros_like(n_s)
        m_s[...] = jnp.zeros_like(m_s)

    ncb = gcol_ref.shape[1]
    row = lax.broadcasted_iota(jnp.int32, (chunk, chunk), 0)
    col = lax.broadcasted_iota(jnp.int32, (chunk, chunk), 1)
    mask = (col >= row) if reverse else (col <= row)
    mask_t = (row >= col) if reverse else (row <= col)
    scale = dqk ** -0.5

    def step(ci, carry):
        cc = (ncb - 1 - ci) if reverse else ci
        r0 = pl.multiple_of(cc * chunk, chunk)
        gcol = gcol_ref[0, cc] + bcol_ref[...]
        grow = grow_ref[0, cc] + brow_ref[...]
        fcol = -_softplus(-gcol)
        frow = -_softplus(-grow)
        hs = range(heads)
        i_col = [gcol[:, h:h + 1] for h in hs]
        f_col = [fcol[:, heads + h:heads + h + 1] for h in hs]
        i_row = [grow[h:h + 1, :] for h in hs]
        f_row = [frow[heads + h:heads + h + 1, :] for h in hs]
        b_col = [jnp.sum(jnp.where(mask, f_row[h], 0.0), axis=1, keepdims=True) for h in hs]
        b_row = [jnp.sum(jnp.where(mask_t, f_col[h], 0.0), axis=0, keepdims=True) for h in hs]
        total = [jnp.sum(f_row[h], axis=1, keepdims=True) for h in hs]
        m_prev = [m_s[h] for h in hs]
        q = [q_ref[0, pl.ds(r0, chunk), h * dqk:(h + 1) * dqk] * scale for h in hs]
        k = [k_ref[0, pl.ds(r0, chunk), h * dqk:(h + 1) * dqk] for h in hs]
        vb = [v_ref[0, pl.ds(r0, chunk), h * dv:(h + 1) * dv].astype(BF16) for h in hs]
        c_prev = [c_s[h] for h in hs]
        n_prev = [n_s[h] for h in hs]
        qkc = [_mm(q[h].astype(BF16),
                   jnp.concatenate([k[h].T.astype(BF16), c_prev[h].astype(BF16)], axis=1))
               for h in hs]
        dmat = [jnp.where(mask, b_col[h] + (i_row[h] - b_row[h]), -jnp.inf) for h in hs]
        inter = [b_col[h] + m_prev[h] for h in hs]
        m_row = [jnp.maximum(inter[h], jnp.max(dmat[h], axis=1, keepdims=True)) for h in hs]
        w_inter = [jnp.exp(inter[h] - m_row[h]) for h in hs]
        sqk = [qkc[h][:, :chunk] * jnp.exp(dmat[h] - m_row[h]) for h in hs]
        dlast = [total[h] + (i_col[h] - b_col[h]) for h in hs]
        m_new = [jnp.maximum(total[h] + m_prev[h], jnp.max(dlast[h], axis=0, keepdims=True))
                 for h in hs]
        kw = [k[h] * jnp.exp(dlast[h] - m_new[h]) for h in hs]
        both = [_mm(jnp.concatenate([sqk[h].astype(BF16), kw[h].T.astype(BF16)], axis=0), vb[h])
                for h in hs]
        for h in hs:
            num = w_inter[h] * qkc[h][:, chunk:] + both[h][:chunk]
            den = (w_inter[h] * jnp.sum(q[h] * n_prev[h], axis=1, keepdims=True)
                   + jnp.sum(sqk[h], axis=1, keepdims=True))
            out = num / jnp.maximum(jnp.abs(den), jnp.exp(-m_row[h]))
            if add_ref is not None:
                out = add_ref[0, pl.ds(r0, chunk), h * dv:(h + 1) * dv] + out
            h_ref[0, pl.ds(r0, chunk), h * dv:(h + 1) * dv] = out
            decay = jnp.exp(total[h] + m_prev[h] - m_new[h])
            c_s[h] = decay * c_prev[h] + both[h][chunk:]
            n_s[h] = decay * n_prev[h] + jnp.sum(kw[h], axis=0, keepdims=True)
            m_s[h] = m_new[h]
        return carry

    lax.fori_loop(0, ncb, step, 0)


def _mlstm_scan(proj, gcol, grow, bcol, brow, *, reverse, add=None):
    b, s, _ = proj.shape
    heads = A_HEADS
    dqk = proj.shape[2] // (6 * heads)
    dv = 2 * dqk
    nq, nv = heads * dqk, heads * dv
    blk = min(SCAN_BLOCK, s)
    nb = s // blk
    chunk = gcol.shape[2]
    ncb = blk // chunk
    ng = gcol.shape[3]

    def seq(j):
        return (nb - 1 - j) if reverse else j

    out_spec = pl.BlockSpec((1, blk, nv), lambda i, j: (i, seq(j), 0))
    extra = [] if add is None else [add]
    return pl.pallas_call(
        functools.partial(_mlstm_kernel, reverse=reverse, chunk=chunk, heads=heads,
                          dqk=dqk, dv=dv),
        out_shape=jax.ShapeDtypeStruct((b, s, nv), F32),
        grid=(b, nb),
        in_specs=[pl.BlockSpec((1, blk, nq), lambda i, j: (i, seq(j), 0)),
                  pl.BlockSpec((1, blk, nq), lambda i, j: (i, seq(j), 1)),
                  pl.BlockSpec((1, blk, nv), lambda i, j: (i, seq(j), 1)),
                  pl.BlockSpec((1, ncb, chunk, ng), lambda i, j: (i, seq(j), 0, 0)),
                  pl.BlockSpec((1, ncb, ng, chunk), lambda i, j: (i, seq(j), 0, 0)),
                  pl.BlockSpec((1, ng), lambda i, j: (0, 0)),
                  pl.BlockSpec((ng, 1), lambda i, j: (0, 0))] + [out_spec] * len(extra),
        out_specs=out_spec,
        scratch_shapes=[pltpu.VMEM((heads, dqk, dv), F32),
                        pltpu.VMEM((heads, 1, dqk), F32),
                        pltpu.VMEM((heads, 1, 1), F32)],
        compiler_params=_params("parallel", "arbitrary"),
        name="mlstm_scan_bwd" if reverse else "mlstm_scan_fwd",
    )(proj, proj, proj, gcol, grow, bcol, brow, *extra)


def _seq_halo_specs(tm, s, width, col_block=0):
    r = tm // HALO
    last = s // HALO - 1

    def prev(i, j):
        return (i, jnp.maximum(j * r - 1, 0), col_block)

    def nxt(i, j):
        return (i, jnp.minimum((j + 1) * r, last), col_block)

    return [pl.BlockSpec((1, HALO, width), prev),
            pl.BlockSpec((1, tm, width), lambda i, j: (i, j, col_block)),
            pl.BlockSpec((1, HALO, width), nxt)]


def _halo_valid(tm):
    j = pl.program_id(1)
    rid = lax.broadcasted_iota(jnp.int32, (tm + 2 * HALO, 1), 0)
    return jnp.logical_and(jnp.logical_or(rid >= HALO, j > 0),
                           jnp.logical_or(rid < tm + HALO, j < pl.num_programs(1) - 1))


def _dwconv_rows(ext, w_ref, c0, c1, taps, tm):
    n = ext.shape[0]
    acc = None
    for t in range(taps):
        off = t - taps // 2
        src = ext if off == 0 else pltpu.roll(ext, (-off) % n, 0)
        term = src * w_ref[t:t + 1, c0:c1]
        acc = term if acc is None else acc + term
    return acc[HALO:HALO + tm]


def _gdn_inproj_kernel(xp_ref, x_ref, xn_ref, sh_ref, sc_ref, w_ref, cw_ref, qkv_ref, z_ref, g_ref,
                       *, tm, heads, dk, tn):
    ext = jnp.concatenate([xp_ref[0], x_ref[0], xn_ref[0]], axis=0)
    hn = _rms_rows(ext) * (1.0 + sc_ref[0]) + sh_ref[0]
    hn = jnp.where(_halo_valid(tm), hn, 0.0).astype(BF16)
    hw = heads * dk
    for c0 in range(0, 3 * hw, tn):
        y = _dwconv_rows(_mm(hn, w_ref[:, c0:c0 + tn]), cw_ref, c0, c0 + tn, B_CONV, tm)
        y = y * _sigmoid(y)
        for g0 in range(0, tn, dk):
            yh = y[:, g0:g0 + dk]
            if c0 < 2 * hw:
                yh = yh * lax.rsqrt(jnp.sum(yh * yh, axis=-1, keepdims=True) + EPS)
            if c0 < hw:
                yh = yh * (dk ** -0.5)
            qkv_ref[0, :, c0 + g0:c0 + g0 + dk] = yh
    hm = hn[HALO:HALO + tm]
    for c0 in range(0, hw, tn):
        z_ref[0, :, c0:c0 + tn] = _mm(hm, w_ref[:, 3 * hw + c0:3 * hw + c0 + tn])
    g_ref[0] = _mm(hm, w_ref[:, 4 * hw:])


def _gdn_inproj(x, mod, w, conv_w):
    b, s, d = x.shape
    heads = B_HEADS
    n_gate = LANES
    hw = (w.shape[1] - n_gate) // 4
    tm = min(ROW_TILE, s)
    const = lambda i, j: (0, 0)
    row = lambda n: pl.BlockSpec((1, tm, n), lambda i, j: (i, j, 0))
    return pl.pallas_call(
        functools.partial(_gdn_inproj_kernel, tm=tm, heads=heads, dk=hw // heads, tn=512),
        out_shape=[jax.ShapeDtypeStruct((b, s, 3 * hw), F32),
                   jax.ShapeDtypeStruct((b, s, hw), F32),
                   jax.ShapeDtypeStruct((b, s, n_gate), F32)],
        grid=(b, s // tm),
        in_specs=_seq_halo_specs(tm, s, d) + [
            pl.BlockSpec((1, 1, d), lambda i, j: (i, 0, 0)),
            pl.BlockSpec((1, 1, d), lambda i, j: (i, 0, 1)),
            pl.BlockSpec(w.shape, const),
            pl.BlockSpec(conv_w.shape, const)],
        out_specs=[row(3 * hw), row(hw), row(n_gate)],
        compiler_params=_params("parallel", "parallel"),
        name="gdn_inproj",
    )(x, x, x, mod, mod, w, conv_w)


def _gdn_kernel(q_ref, k_ref, v_ref, gcol_ref, grow_ref, pcol_ref, prow_ref, *rest,
                reverse, chunk, group, heads, dk):
    add_ref = rest[0] if len(rest) == 6 else None
    o_ref, s_s, mq_s, r_s, gl_s = rest[-5:]

    @pl.when(pl.program_id(1) == 0)
    def _():
        s_s[...] = jnp.zeros_like(s_s)

    blk = q_ref.shape[1]
    cpg = group // chunk
    clog = chunk.bit_length() - 1
    row = lax.broadcasted_iota(jnp.int32, (group, group), 0)
    col = lax.broadcasted_iota(jnp.int32, (group, group), 1)

    def same_block(log2):
        return jnp.right_shift(row, log2) == jnp.right_shift(col, log2)

    same_chunk = same_block(clog)
    causal = jnp.logical_and(same_chunk, (col >= row) if reverse else (col <= row))
    causal_t = jnp.logical_and(same_chunk, (row >= col) if reverse else (row <= col))
    strict = jnp.logical_and(same_chunk, (col > row) if reverse else (col < row))
    eye = (row == col).astype(F32)
    base_log2 = 3
    causal_f, causal_tf = causal.astype(F32), causal_t.astype(F32)
    same_chunk_f, strict_f = same_chunk.astype(F32), strict.astype(F32)
    same_base_b = same_block(base_log2).astype(BF16)
    level_masks_b = [jnp.logical_and(same_block(lg + 1),
                                     jnp.logical_not(same_block(lg))).astype(BF16)
                     for lg in range(base_log2, clog)]
    chunk_of_row = jnp.right_shift(lax.broadcasted_iota(jnp.int32, (group, 1), 0), clog)

    raw_col = gcol_ref[0]
    raw_row = grow_ref[0]
    g_cols = -jnp.exp(pcol_ref[0:1, :]) * _softplus(raw_col[:, :heads] + pcol_ref[1:2, :])
    g_rows = -jnp.exp(prow_ref[:, 0:1]) * _softplus(raw_row[:heads, :] + prow_ref[:, 1:2])
    beta_cols = _sigmoid(raw_col[:, heads:])

    def par(fn, *lists):
        return [fn(*args) for args in zip(*lists)]

    def chunk_local(units):
        rows = [slice(gi * group, (gi + 1) * group) for _, gi in units]
        cols = [slice(h * dk, (h + 1) * dk) for h, _ in units]
        g_col = [g_cols[r, h:h + 1] for (h, _), r in zip(units, rows)]
        g_row = [g_rows[h:h + 1, r] for (h, _), r in zip(units, rows)]
        beta = [beta_cols[r, h:h + 1] for (h, _), r in zip(units, rows)]
        gc_col = par(lambda g: jnp.sum(causal_f * g, axis=1, keepdims=True), g_row)
        gc_row = par(lambda g: jnp.sum(causal_tf * g, axis=0, keepdims=True), g_col)
        tot_col = par(lambda g: jnp.sum(same_chunk_f * g, axis=1, keepdims=True), g_row)
        decay = par(lambda c, r: jnp.exp(jnp.where(causal, c - r, -jnp.inf)), gc_col, gc_row)
        q = [q_ref[0, r, c] for r, c in zip(rows, cols)]
        k = [k_ref[0, r, c] for r, c in zip(rows, cols)]
        v = [v_ref[0, r, c] for r, c in zip(rows, cols)]
        kb = par(lambda x: x.astype(BF16), k)
        kq = par(lambda kb_, q_: _mm_nt(jnp.concatenate([kb_, q_.astype(BF16)], axis=0), kb_),
                 kb, q)
        a = par(lambda kq_, d, b_: (kq_[:group] * d * strict_f * b_).astype(BF16), kq, decay, beta)
        n1 = par(lambda a_: a_ * same_base_b, a)
        n2 = par(lambda n: _mm_inv(n, n), n1)
        n4 = par(lambda n: _mm_inv(n, n), n2)
        x = par(lambda n, m: _mm_inv(eye - n, eye + m), n1, n2)
        x = par(lambda x_, n: _mm_inv(x_, eye + n), x, n4)
        for lm in level_masks_b:
            xl = par(lambda x_, a_: _mm_inv(x_, a_ * lm), x, a)
            x = par(lambda x_, xl_: x_ - _mm_inv(xl_, x_), x, xl)
        eg = par(jnp.exp, gc_col)
        wu = par(lambda x_, k_, b_, e, v_: _mm_inv(
            x_, jnp.concatenate([k_ * b_ * e, v_ * b_], axis=1)).astype(BF16), x, k, beta, eg, v)
        qo = par(lambda kq_, d, wu_: _mm((kq_[group:] * d).astype(BF16), wu_), kq, decay, wu)
        kd_by_chunk = par(lambda k_, t_, c_: jnp.concatenate(
            [jnp.where(chunk_of_row == c, k_ * jnp.exp(t_ - c_), 0.0) for c in range(cpg)],
            axis=1).astype(BF16), k, tot_col, gc_col)
        mr = par(_mm_tn, kd_by_chunk, wu)
        for i, (h, gi) in enumerate(units):
            o_ref[0, rows[i], cols[i]] = qo[i][:, dk:]
            q_prime = (q[i] * eg[i] - qo[i][:, :dk]).astype(BF16)
            for c in range(cpg):
                cg = gi * cpg + c
                mq_s[h, cg, 0:dk, :] = mr[i][c * dk:(c + 1) * dk, :dk].astype(BF16)
                mq_s[h, cg, dk:dk + chunk, :] = q_prime[c * chunk:(c + 1) * chunk]
                r_s[h, cg] = mr[i][c * dk:(c + 1) * dk, dk:]
                total = jnp.sum(g_row[i][:, c * chunk:(c + 1) * chunk], axis=1, keepdims=True)
                gl_s[h, cg] = jnp.broadcast_to(jnp.exp(total), (1, dk))

    all_units = [(h, gi) for h in range(heads) for gi in range(blk // group)]
    for u0 in range(0, len(all_units), GDN_UNITS):
        chunk_local(all_units[u0:u0 + GDN_UNITS])

    ncb = blk // chunk
    for ci in range(ncb):
        cc = (ncb - 1 - ci) if reverse else ci
        s_prev = [s_s[h] for h in range(heads)]
        p = [_mm(mq_s[h, cc], s_prev[h].astype(BF16)) for h in range(heads)]
        for h in range(heads):
            s_s[h] = gl_s[h, cc] * s_prev[h] - p[h][:dk] + r_s[h, cc]
            tile = (slice(cc * chunk, (cc + 1) * chunk), slice(h * dk, (h + 1) * dk))
            out = o_ref[0, tile[0], tile[1]] + p[h][dk:]
            if add_ref is not None:
                out = add_ref[0, tile[0], tile[1]] + out
            o_ref[0, tile[0], tile[1]] = out


def _gdn_scan(qkv, gcol, grow, pcol, prow, *, reverse, add=None):
    b, s, n = qkv.shape
    heads = B_HEADS
    hw = n // 3
    dk = hw // heads
    blk = min(SCAN_BLOCK, s)
    nb = s // blk
    ncb = blk // SCAN_CHUNK
    ng = gcol.shape[2]

    def seq(j):
        return (nb - 1 - j) if reverse else j

    out_spec = pl.BlockSpec((1, blk, hw), lambda i, j: (i, seq(j), 0))
    extra = [] if add is None else [add]
    return pl.pallas_call(
        functools.partial(_gdn_kernel, reverse=reverse, chunk=SCAN_CHUNK, group=GDN_GROUP,
                          heads=heads, dk=dk),
        out_shape=jax.ShapeDtypeStruct((b, s, hw), F32),
        grid=(b, nb),
        in_specs=[pl.BlockSpec((1, blk, hw), lambda i, j: (i, seq(j), 0)),
                  pl.BlockSpec((1, blk, hw), lambda i, j: (i, seq(j), 1)),
                  pl.BlockSpec((1, blk, hw), lambda i, j: (i, seq(j), 2)),
                  pl.BlockSpec((1, blk, ng), lambda i, j: (i, seq(j), 0)),
                  pl.BlockSpec((1, ng, blk), lambda i, j: (i, 0, seq(j))),
                  pl.BlockSpec((2, heads), lambda i, j: (0, 0)),
                  pl.BlockSpec((heads, 2), lambda i, j: (0, 0))] + [out_spec] * len(extra),
        out_specs=out_spec,
        scratch_shapes=[pltpu.VMEM((heads, dk, dk), F32),
                        pltpu.VMEM((heads, ncb, dk + SCAN_CHUNK, dk), BF16),
                        pltpu.VMEM((heads, ncb, dk, dk), F32),
                        pltpu.VMEM((heads, ncb, 1, dk), F32)],
        compiler_params=_params("parallel", "arbitrary"),
        name="gdn_scan_bwd" if reverse else "gdn_scan_fwd",
    )(qkv, qkv, qkv, gcol, grow, pcol, prow, *extra)


def _rel_bucket(rel):
    nb = REL_BUCKETS // 2
    exact = nb // 2
    n = jnp.abs(rel)
    large = exact + (jnp.log(jnp.maximum(n, 1).astype(jnp.float32) / exact)
                     / math.log(REL_MAX_DIST / exact) * (nb - exact)).astype(jnp.int32)
    large = jnp.minimum(large, nb - 1)
    return jnp.where(rel > 0, nb, 0) + jnp.where(n < exact, n, large)


def _bias_tiles(rel_bias, t):
    assert t >= REL_MAX_DIST
    n = 2 * t + 1
    m = jnp.arange(n, dtype=jnp.int32)
    rel_in_tile = jnp.where(m < t, m, m - n)
    d = jnp.arange(-2, 3, dtype=jnp.int32)
    rel = d[:, None] * t + rel_in_tile[None, :]
    vals = jnp.transpose(rel_bias.astype(F32)[_rel_bucket(rel)], (2, 0, 1)) * LOG2E
    h = vals.shape[0]
    tiled = jnp.tile(vals, (1, 1, t))[:, :, :t * (n - 1)].reshape(h, 5, t, n - 1)
    return tiled[:, :, :, :t]


def _attn_kernel(q_ref, k_ref, v_ref, bias_ref, lam_ref, ng_ref, o_ref,
                 m1_s, a1_s, m2_s, a2_s, vext_s, *, lambda_init, dh):
    qi = pl.program_id(2)
    ki = pl.program_id(3)
    t = bias_ref.shape[2]
    nq = q_ref.shape[1] // t
    tk, dv = v_ref.shape[1], v_ref.shape[2]
    r = tk // t
    maps = ((m1_s, a1_s), (m2_s, a2_s))

    @pl.when(ki == 0)
    def _():
        for m_s, a_s in maps:
            m_s[...] = jnp.full_like(m_s, -jnp.inf)
            a_s[...] = jnp.zeros_like(a_s)
        vext_s[:, dv:] = jnp.ones((tk, dv), BF16)

    vext_s[:, :dv] = v_ref[0]
    k_maps = [k_ref[0, :, i * dh:(i + 1) * dh] for i in range(2)]
    first = ki * r - qi * nq

    def update(with_bias):
        def logits(iq):
            rows = slice(iq * t, (iq + 1) * t)
            s = [_mm_nt(q_ref[0, rows, i * dh:(i + 1) * dh], k_maps[i]) for i in range(2)]
            if with_bias:
                bias = jnp.concatenate(
                    [bias_ref[0, jnp.clip(first - iq + j, -2, 2) + 2] for j in range(r)], axis=1)
                s = [s_ + bias for s_ in s]
            return s

        s_next = logits(0)
        for iq in range(nq):
            rows = slice(iq * t, (iq + 1) * t)
            s = s_next
            if iq + 1 < nq:
                s_next = logits(iq + 1)
            shift = 0.0 if with_bias else bias_ref[0, jnp.clip(first - iq, -2, 2) + 2, 0:1, 0:1]
            m_prev = [m_s[rows] for m_s, _ in maps]
            m_new = [jnp.maximum(m_prev[i], jnp.max(s[i], axis=1, keepdims=True) + shift)
                     for i in range(2)]
            p = [jnp.exp2(s[i] - (m_new[i] - shift)).astype(BF16) for i in range(2)]
            pv = [_mm(p[i], vext_s[...]) for i in range(2)]
            for i, (m_s, a_s) in enumerate(maps):
                a_s[rows] = jnp.exp2(m_prev[i] - m_new[i]) * a_s[rows] + pv[i]
                m_s[rows] = m_new[i]

    far = jnp.logical_or(first - (nq - 1) >= 2, first + (r - 1) <= -2)

    @pl.when(far)
    def _():
        update(False)

    @pl.when(jnp.logical_not(far))
    def _():
        update(True)

    @pl.when(ki == pl.num_programs(3) - 1)
    def _():
        lp = lam_ref[...]
        lam = (jnp.exp(jnp.sum(lp[0:1] * lp[1:2], axis=1, keepdims=True))
               - jnp.exp(jnp.sum(lp[2:3] * lp[3:4], axis=1, keepdims=True)) + lambda_init)
        a1, a2 = a1_s[...], a2_s[...]
        o = a1[:, :dv] / a1[:, dv:dv + 1] - lam * (a2[:, :dv] / a2[:, dv:dv + 1])
        o_ref[0] = _rms_rows(o) * ng_ref[...] * (1.0 - lambda_init)


def _diff_attn(proj, bias, lam_p, norm_g, lambda_init):
    b, s, n = proj.shape
    heads = C_HEADS
    dv = n // (3 * heads)
    t = bias.shape[2]
    tk = min(ATTN_KEYS, s)
    tq = min(ATTN_ROWS, s)
    return pl.pallas_call(
        functools.partial(_attn_kernel, lambda_init=lambda_init, dh=dv // 2),
        out_shape=jax.ShapeDtypeStruct((b, s, heads * dv), F32),
        grid=(b, heads, s // tq, s // tk),
        in_specs=[pl.BlockSpec((1, tq, dv), lambda i, h, qi, ki: (i, qi, h)),
                  pl.BlockSpec((1, tk, dv), lambda i, h, qi, ki: (i, ki, heads + h)),
                  pl.BlockSpec((1, tk, dv), lambda i, h, qi, ki: (i, ki, 2 * heads + h)),
                  pl.BlockSpec((1, 5, t, t), lambda i, h, qi, ki: (h, 0, 0, 0)),
                  pl.BlockSpec(lam_p.shape, lambda i, h, qi, ki: (0, 0)),
                  pl.BlockSpec((1, dv), lambda i, h, qi, ki: (0, 0))],
        out_specs=pl.BlockSpec((1, tq, dv), lambda i, h, qi, ki: (i, qi, h)),
        scratch_shapes=[pltpu.VMEM((tq, 1), F32), pltpu.VMEM((tq, 2 * dv), F32),
                        pltpu.VMEM((tq, 1), F32), pltpu.VMEM((tq, 2 * dv), F32),
                        pltpu.VMEM((tk, 2 * dv), BF16)],
        compiler_params=_params("parallel", "parallel", "parallel", "arbitrary"),
        name="diff_attn",
    )(proj, proj, proj, bias, lam_p, norm_g.reshape(1, dv))


def _outproj_kernel(*refs, mode, hd):
    if mode == "attn":
        a_ref, w_ref, x_ref, g_ref, o_ref = refs
        act = a_ref[0]
    else:
        a_ref, gate_ref, ng_ref, w_ref, x_ref, g_ref, o_ref = refs
        ssum = a_ref[0]
        d = ssum.shape[1]
        hs = jnp.concatenate([_rms_rows(ssum[:, c0:c0 + hd]) for c0 in range(0, d, hd)], axis=1)
        hs = hs * ng_ref[...]
        gate = gate_ref[0]
        if mode == "mlstm":
            act = _sigmoid(gate) * hs
        else:
            act = hs * (gate * _sigmoid(gate))
    o_ref[0] = x_ref[0] + g_ref[0] * _mm(act.astype(BF16), w_ref[...])


def _outproj(mode, x, mod, w, a, gate_src=None, gate_block=0, norm_g=None, hd=0):
    bsz, s, d = x.shape
    tm = min(ROW_TILE, s)
    row = pl.BlockSpec((1, tm, d), lambda i, j: (i, j, 0))
    in_specs, args = [row], [a]
    if mode != "attn":
        in_specs += [pl.BlockSpec((1, tm, d), lambda i, j: (i, j, gate_block)),
                     pl.BlockSpec((1, d), lambda i, j: (0, 0))]
        args += [gate_src, norm_g.reshape(1, d)]
    in_specs += [pl.BlockSpec((d, d), lambda i, j: (0, 0)), row,
                 pl.BlockSpec((1, 1, d), lambda i, j: (i, 0, 2))]
    args += [w, x, mod]
    return pl.pallas_call(
        functools.partial(_outproj_kernel, mode=mode, hd=hd),
        out_shape=jax.ShapeDtypeStruct((bsz, s, d), F32),
        grid=(bsz, s // tm),
        in_specs=in_specs,
        out_specs=row,
        compiler_params=_params("parallel", "parallel"),
        name="outproj_" + mode,
    )(*args)


def _ffn_kernel(*refs, tm, dff, tn, final):
    if final:
        xp_ref, x_ref, xn_ref, sh_ref, sc_ref, g_ref, wu_ref, cw_ref, cb_ref, wd_ref, fg_ref, o_ref = refs
    else:
        xp_ref, x_ref, xn_ref, sh_ref, sc_ref, g_ref, wu_ref, cw_ref, cb_ref, wd_ref, o_ref = refs
    x = x_ref[0]
    ext = jnp.concatenate([xp_ref[0], x, xn_ref[0]], axis=0)
    hn = _rms_rows(ext) * (1.0 + sc_ref[0]) + sh_ref[0]
    hn = jnp.where(_halo_valid(tm), hn, 0.0).astype(BF16)
    acc = jnp.zeros_like(x)
    for c0 in range(0, dff, tn):
        c1 = min(c0 + tn, dff)
        ua = _dwconv_rows(_mm(hn, wu_ref[:, c0:c1]), cw_ref, c0, c1, FFN_CONV, tm)
        ug = _dwconv_rows(_mm(hn, wu_ref[:, dff + c0:dff + c1]), cw_ref, dff + c0, dff + c1,
                          FFN_CONV, tm)
        ua = ua + cb_ref[:, c0:c1]
        ug = ug + cb_ref[:, dff + c0:dff + c1]
        act = ua * (ug * _sigmoid(ug))
        acc = acc + _mm(act.astype(BF16), wd_ref[c0:c1, :])
    y = x + g_ref[0] * acc
    if final:
        y = _rms_rows(y) * fg_ref[...]
    o_ref[0] = y


def _conv_ffn(x, mod, w_up, conv_w, conv_b, w_down, final_g=None):
    b, s, d = x.shape
    dff = w_down.shape[0]
    tm = min(ROW_TILE, s)
    final = final_g is not None
    const = lambda i, j: (0, 0)
    in_specs = _seq_halo_specs(tm, s, d) + [
        pl.BlockSpec((1, 1, d), lambda i, j: (i, 0, 3)),
        pl.BlockSpec((1, 1, d), lambda i, j: (i, 0, 4)),
        pl.BlockSpec((1, 1, d), lambda i, j: (i, 0, 5)),
        pl.BlockSpec(w_up.shape, const),
        pl.BlockSpec(conv_w.shape, const),
        pl.BlockSpec((1, 2 * dff), const),
        pl.BlockSpec(w_down.shape, const)]
    args = [x, x, x, mod, mod, mod, w_up, conv_w, conv_b.reshape(1, 2 * dff), w_down]
    if final:
        in_specs.append(pl.BlockSpec((1, d), const))
        args.append(final_g.reshape(1, d))
    return pl.pallas_call(
        functools.partial(_ffn_kernel, tm=tm, dff=dff, tn=FFN_COLS, final=final),
        out_shape=jax.ShapeDtypeStruct((b, s, d), F32),
        grid=(b, s // tm),
        in_specs=in_specs,
        out_specs=pl.BlockSpec((1, tm, d), lambda i, j: (i, j, 0)),
        compiler_params=_params("parallel", "parallel"),
        name="conv_ffn",
    )(*args)


def _mlstm_layer(x, mod, w_in, gate_b, norm_g, w_out):
    heads = A_HEADS
    n_main = w_in.shape[1] - 4 * heads
    proj, gates = _inproj(x, mod, _split_w_in(w_in, n_main), n_main)
    hsum = None
    for direction in range(2):
        lo = 2 * heads * direction
        gcol, grow = _gate_layouts(gates, lo, 2 * heads, min(MLSTM_CHUNK, x.shape[1]))
        bias = gate_b[lo:lo + 2 * heads].astype(F32)
        hsum = _mlstm_scan(proj, gcol, grow, bias.reshape(1, -1), bias.reshape(-1, 1),
                           reverse=direction == 1, add=hsum)
    d = x.shape[2]
    return _outproj("mlstm", x, mod, w_out.astype(BF16), hsum, gate_src=proj,
                    gate_block=n_main // d - 1, norm_g=norm_g, hd=d // heads)


def _gdn_layer(x, mod, w_in, conv_w, a_log, dt_bias, norm_g, w_out):
    heads = B_HEADS
    n_main = w_in.shape[1] - 4 * heads
    qkv, z, gates = _gdn_inproj(x, mod, _split_w_in(w_in, n_main), conv_w)
    osum = None
    for direction in range(2):
        lo = 2 * heads * direction
        gcol = gates[:, :, lo:lo + 2 * heads]
        grow = jnp.swapaxes(gcol, 1, 2)
        p = jnp.stack([a_log[direction], dt_bias[direction]]).astype(F32)
        osum = _gdn_scan(qkv, gcol, grow, p, p.T, reverse=direction == 1, add=osum)
    d = x.shape[2]
    return _outproj("gdn", x, mod, w_out.astype(BF16), osum, gate_src=z,
                    gate_block=0, norm_g=jnp.tile(norm_g, heads), hd=d // heads)


def _attn_layer(x, mod, w_in, lam_p, norm_g, w_out, bias, lambda_init):
    n = w_in.shape[1]
    dh = n // (6 * C_HEADS)
    proj = _inproj(x, mod, w_in.astype(BF16), n, out_dtype=BF16, lead_cols=n // 3,
                   lead_scale=dh ** -0.5 * LOG2E)[0]
    o = _diff_attn(proj, bias, lam_p.astype(F32), norm_g, lambda_init)
    return _outproj("attn", x, mod, w_out.astype(BF16), o)


def _trunk(x, mods, bias_tiles, ada_w, ada_b, a_w_in, a_gate_b, a_norm_g, a_w_out, b_w_in, b_conv_w,
           b_a_log, b_dt_bias, b_norm_g, b_w_out, c_w_in, c_lambda, c_norm_g, c_w_out, rel_bias,
           ffn_w_up, ffn_conv_w, ffn_conv_b, ffn_w_down, final_g):
    depth = ada_w.shape[0]
    for i in range(depth):
        mod = mods[i][:, None, :]
        j = i // N_MIXERS
        kind = i % N_MIXERS
        if kind == 0:
            x = _mlstm_layer(x, mod, a_w_in[j], a_gate_b[j], a_norm_g[j], a_w_out[j])
        elif kind == 1:
            x = _gdn_layer(x, mod, b_w_in[j], b_conv_w[j], b_a_log[j], b_dt_bias[j], b_norm_g[j],
                           b_w_out[j])
        else:
            x = _attn_layer(x, mod, c_w_in[j], c_lambda[j], c_norm_g[j], c_w_out[j], bias_tiles,
                            0.8 - 0.6 * math.exp(-0.3 * i))
        x = _conv_ffn(x, mod, ffn_w_up[i].astype(BF16), ffn_conv_w[i], ffn_conv_b[i],
                      ffn_w_down[i].astype(BF16), final_g if i == depth - 1 else None)
    return x


def kernel(x_prompt, x_sample, c_prompt, c_sample, ada_w, ada_b, a_w_in, a_gate_b, a_norm_g, a_w_out, b_w_in, b_conv_w, b_a_log, b_dt_bias, b_norm_g, b_w_out, c_w_in, c_lambda, c_norm_g, c_w_out, rel_bias, ffn_w_up, ffn_conv_w, ffn_conv_b, ffn_w_down, final_g):
    weights = (ada_w, ada_b, a_w_in, a_gate_b, a_norm_g, a_w_out, b_w_in, b_conv_w, b_a_log,
               b_dt_bias, b_norm_g, b_w_out, c_w_in, c_lambda, c_norm_g, c_w_out, rel_bias,
               ffn_w_up, ffn_conv_w, ffn_conv_b, ffn_w_down, final_g)
    nb_p, nb_s = c_prompt.shape[0], c_sample.shape[0]
    c_all = jnp.concatenate([c_prompt, c_sample], axis=0)
    pad = (-c_all.shape[0]) % SUBLANES
    mods = _ada_mod(jnp.pad(c_all, ((0, pad), (0, 0))), ada_w, ada_b)
    outs, bias_tiles = [], {}
    for x, lo, n in ((x_prompt, 0, nb_p), (x_sample, nb_p, nb_s)):
        t = min(ATTN_TILE, x.shape[1])
        if t not in bias_tiles:
            bias_tiles[t] = _bias_tiles(rel_bias, t)
        outs.append(_trunk(x, mods[:, lo:lo + n], bias_tiles[t], *weights))
    return tuple(outs)
```

```python
import functools
import math

import jax
import jax.numpy as jnp
from jax import lax
from jax.experimental import pallas as pl
from jax.experimental.pallas import tpu as pltpu

F32 = jnp.float32
BF16 = jnp.bfloat16
EPS = 1e-6
LOG2E = math.log2(math.e)

N_MIXERS = 3
A_HEADS = 4
B_HEADS = 8
B_CONV = 5
C_HEADS = 8
REL_BUCKETS = 32
REL_MAX_DIST = 128
FFN_CONV = 3

LANES = 128
SUBLANES = 8
HALO = SUBLANES
VMEM_LIMIT = 56 * 1024 * 1024

ROW_TILE = 512
OUTPROJ_ROWS = 1024
SCAN_BLOCK = 512
MLSTM_BLOCK = 1024
SCAN_CHUNK = 64
GDN_GROUP = 256
GDN_UNITS = 4
MLSTM_CHUNK = 256
ATTN_TILE = 512
ATTN_ROWS = 2048
ATTN_KEYS = 2048
MXU_WIDTH = 256
FFN_COLS = 6 * MXU_WIDTH


def _params(*sem):
    return pltpu.CompilerParams(dimension_semantics=sem, vmem_limit_bytes=VMEM_LIMIT)


def _sigmoid(x):
    return 1.0 / (1.0 + jnp.exp(-x))


def _softplus(x):
    return jnp.maximum(x, 0.0) + jnp.log(1.0 + jnp.exp(-jnp.abs(x)))


def _rms_rows(x):
    return x * lax.rsqrt(jnp.mean(x * x, axis=-1, keepdims=True) + EPS)


def _mm(a, b):
    return jnp.dot(a, b, preferred_element_type=F32)


def _mm_inv(a, b):
    return _mm(a.astype(BF16), b.astype(BF16))


def _mm_nt(a, b):
    return lax.dot_general(a, b, (((1,), (1,)), ((), ())), preferred_element_type=F32)


def _mm_tn(a, b):
    return lax.dot_general(a, b, (((0,), (0,)), ((), ())), preferred_element_type=F32)


def _mod_kernel(c_ref, w_ref, b_ref, o_ref):
    c = c_ref[...]
    cs = c * _sigmoid(c)
    o_ref[0] = _mm(cs.astype(BF16), w_ref[0].astype(BF16)) + b_ref[0]


def _ada_mod(c, ada_w, ada_b):
    depth, d, n = ada_w.shape
    bp = c.shape[0]
    tn = 1536
    return pl.pallas_call(
        _mod_kernel,
        out_shape=jax.ShapeDtypeStruct((depth, bp, n), F32),
        grid=(depth, n // tn),
        in_specs=[pl.BlockSpec((bp, d), lambda l, j: (0, 0)),
                  pl.BlockSpec((1, d, tn), lambda l, j: (l, 0, j)),
                  pl.BlockSpec((1, 1, tn), lambda l, j: (l, 0, j))],
        out_specs=pl.BlockSpec((1, bp, tn), lambda l, j: (l, 0, j)),
        compiler_params=_params("parallel", "parallel"),
        name="ada_mod",
    )(c, ada_w, ada_b.reshape(depth, 1, n))


def _inproj_kernel(x_ref, sh_ref, sc_ref, w_ref, *o_refs, n_main, tn, lead_cols, lead_scale):
    hn = (_rms_rows(x_ref[0]) * (1.0 + sc_ref[0]) + sh_ref[0]).astype(BF16)
    for c0 in range(0, n_main, tn):
        y = _mm(hn, w_ref[:, c0:c0 + tn])
        if c0 < lead_cols:
            y = y * lead_scale
        o_refs[0][0, :, c0:c0 + tn] = y.astype(o_refs[0].dtype)
    if len(o_refs) > 1:
        o_refs[1][0] = _mm(hn, w_ref[:, n_main:])


def _inproj(x, mod, w, n_main, out_dtype=F32, lead_cols=0, lead_scale=1.0):
    b, s, d = x.shape
    n_all = w.shape[1]
    tm = min(ROW_TILE, s)
    tn = 512
    assert lead_cols % tn == 0
    out_shape = [jax.ShapeDtypeStruct((b, s, n_main), out_dtype)]
    out_specs = [pl.BlockSpec((1, tm, n_main), lambda i, j: (i, j, 0))]
    if n_all > n_main:
        out_shape.append(jax.ShapeDtypeStruct((b, s, n_all - n_main), F32))
        out_specs.append(pl.BlockSpec((1, tm, n_all - n_main), lambda i, j: (i, j, 0)))
    return pl.pallas_call(
        functools.partial(_inproj_kernel, n_main=n_main, tn=tn, lead_cols=lead_cols,
                          lead_scale=lead_scale),
        out_shape=out_shape,
        grid=(b, s // tm),
        in_specs=[pl.BlockSpec((1, tm, d), lambda i, j: (i, j, 0)),
                  pl.BlockSpec((1, 1, d), lambda i, j: (i, 0, 0)),
                  pl.BlockSpec((1, 1, d), lambda i, j: (i, 0, 1)),
                  pl.BlockSpec((d, n_all), lambda i, j: (0, 0))],
        out_specs=out_specs,
        compiler_params=_params("parallel", "parallel"),
        name="inproj",
    )(x, mod, mod, w)


def _split_w_in(w, n_main):
    d, n = w.shape
    if n == n_main:
        return w.astype(BF16)
    pad = LANES - (n - n_main)
    return jnp.pad(w, ((0, 0), (0, pad))).astype(BF16)


def _gate_layouts(g, lo, width, chunk):
    b, s, _ = g.shape
    col = g[:, :, lo:lo + width].reshape(b, s // chunk, chunk, width)
    return col, jnp.swapaxes(col, 2, 3)


def _mlstm_kernel(q_ref, k_ref, v_ref, gcol_ref, grow_ref, bcol_ref, brow_ref, *rest,
                  reverse, chunk, heads, dqk, dv):
    add_ref = rest[0] if len(rest) == 5 else None
    h_ref, c_s, n_s, m_s = rest[-4:]

    @pl.when(pl.program_id(1) == 0)
    def _():
        c_s[...] = jnp.zeros_like(c_s)
        n_s[...] = jnp.zeros_like(n_s)
        m_s[...] = jnp.zeros_like(m_s)

    ncb = gcol_ref.shape[1]
    row = lax.broadcasted_iota(jnp.int32, (chunk, chunk), 0)
    col = lax.broadcasted_iota(jnp.int32, (chunk, chunk), 1)
    mask = (col >= row) if reverse else (col <= row)
    mask_t = (row >= col) if reverse else (row <= col)
    scale = dqk ** -0.5

    def step(ci, carry):
        cc = (ncb - 1 - ci) if reverse else ci
        r0 = pl.multiple_of(cc * chunk, chunk)
        gcol = gcol_ref[0, cc] + bcol_ref[...]
        grow = grow_ref[0, cc] + brow_ref[...]
        fcol = -_softplus(-gcol)
        frow = -_softplus(-grow)
        hs = range(heads)
        i_col = [gcol[:, h:h + 1] for h in hs]
        f_col = [fcol[:, heads + h:heads + h + 1] for h in hs]
        i_row = [grow[h:h + 1, :] for h in hs]
        f_row = [frow[heads + h:heads + h + 1, :] for h in hs]
        b_col = [jnp.sum(jnp.where(mask, f_row[h], 0.0), axis=1, keepdims=True) for h in hs]
        b_row = [jnp.sum(jnp.where(mask_t, f_col[h], 0.0), axis=0, keepdims=True) for h in hs]
        total = [jnp.sum(f_row[h], axis=1, keepdims=True) for h in hs]
        m_prev = [m_s[h] for h in hs]
        q = [q_ref[0, pl.ds(r0, chunk), h * dqk:(h + 1) * dqk] * scale for h in hs]
        k = [k_ref[0, pl.ds(r0, chunk), h * dqk:(h + 1) * dqk] for h in hs]
        vb = [v_ref[0, pl.ds(r0, chunk), h * dv:(h + 1) * dv].astype(BF16) for h in hs]
        c_prev = [c_s[h] for h in hs]
        n_prev = [n_s[h] for h in hs]
        qkc = [_mm(q[h].astype(BF16),
                   jnp.concatenate([k[h].T.astype(BF16), c_prev[h].astype(BF16)], axis=1))
               for h in hs]
        dmat = [jnp.where(mask, b_col[h] + (i_row[h] - b_row[h]), -jnp.inf) for h in hs]
        inter = [b_col[h] + m_prev[h] for h in hs]
        m_row = [jnp.maximum(inter[h], jnp.max(dmat[h], axis=1, keepdims=True)) for h in hs]
        w_inter = [jnp.exp(inter[h] - m_row[h]) for h in hs]
        sqk = [qkc[h][:, :chunk] * jnp.exp(dmat[h] - m_row[h]) for h in hs]
        dlast = [total[h] + (i_col[h] - b_col[h]) for h in hs]
        m_new = [jnp.maximum(total[h] + m_prev[h], jnp.max(dlast[h], axis=0, keepdims=True))
                 for h in hs]
        kw = [k[h] * jnp.exp(dlast[h] - m_new[h]) for h in hs]
        both = [_mm(jnp.concatenate([sqk[h].astype(BF16), kw[h].T.astype(BF16)], axis=0), vb[h])
                for h in hs]
        for h in hs:
            num = w_inter[h] * qkc[h][:, chunk:] + both[h][:chunk]
            den = (w_inter[h] * jnp.sum(q[h] * n_prev[h], axis=1, keepdims=True)
                   + jnp.sum(sqk[h], axis=1, keepdims=True))
            out = num / jnp.maximum(jnp.abs(den), jnp.exp(-m_row[h]))
            if add_ref is not None:
                out = add_ref[0, pl.ds(r0, chunk), h * dv:(h + 1) * dv] + out
            h_ref[0, pl.ds(r0, chunk), h * dv:(h + 1) * dv] = out
            decay = jnp.exp(total[h] + m_prev[h] - m_new[h])
            c_s[h] = decay * c_prev[h] + both[h][chunk:]
            n_s[h] = decay * n_prev[h] + jnp.sum(kw[h], axis=0, keepdims=True)
            m_s[h] = m_new[h]
        return carry

    lax.fori_loop(0, ncb, step, 0)


def _mlstm_scan(proj, gcol, grow, bcol, brow, *, reverse, add=None):
    b, s, _ = proj.shape
    heads = A_HEADS
    dqk = proj.shape[2] // (6 * heads)
    dv = 2 * dqk
    nq, nv = heads * dqk, heads * dv
    blk = min(MLSTM_BLOCK, s)
    nb = s // blk
    chunk = gcol.shape[2]
    ncb = blk // chunk
    ng = gcol.shape[3]

    def seq(j):
        return (nb - 1 - j) if reverse else j

    out_spec = pl.BlockSpec((1, blk, nv), lambda i, j: (i, seq(j), 0))
    extra = [] if add is None else [add]
    return pl.pallas_call(
        functools.partial(_mlstm_kernel, reverse=reverse, chunk=chunk, heads=heads,
                          dqk=dqk, dv=dv),
        out_shape=jax.ShapeDtypeStruct((b, s, nv), F32),
        grid=(b, nb),
        in_specs=[pl.BlockSpec((1, blk, nq), lambda i, j: (i, seq(j), 0)),
                  pl.BlockSpec((1, blk, nq), lambda i, j: (i, seq(j), 1)),
                  pl.BlockSpec((1, blk, nv), lambda i, j: (i, seq(j), 1)),
                  pl.BlockSpec((1, ncb, chunk, ng), lambda i, j: (i, seq(j), 0, 0)),
                  pl.BlockSpec((1, ncb, ng, chunk), lambda i, j: (i, seq(j), 0, 0)),
                  pl.BlockSpec((1, ng), lambda i, j: (0, 0)),
                  pl.BlockSpec((ng, 1), lambda i, j: (0, 0))] + [out_spec] * len(extra),
        out_specs=out_spec,
        scratch_shapes=[pltpu.VMEM((heads, dqk, dv), F32),
                        pltpu.VMEM((heads, 1, dqk), F32),
                        pltpu.VMEM((heads, 1, 1), F32)],
        compiler_params=_params("parallel", "arbitrary"),
        name="mlstm_scan_bwd" if reverse else "mlstm_scan_fwd",
    )(proj, proj, proj, gcol, grow, bcol, brow, *extra)


def _seq_halo_specs(tm, s, width, col_block=0):
    r = tm // HALO
    last = s // HALO - 1

    def prev(i, j):
        return (i, jnp.maximum(j * r - 1, 0), col_block)

    def nxt(i, j):
        return (i, jnp.minimum((j + 1) * r, last), col_block)

    return [pl.BlockSpec((1, HALO, width), prev),
            pl.BlockSpec((1, tm, width), lambda i, j: (i, j, col_block)),
            pl.BlockSpec((1, HALO, width), nxt)]


def _halo_valid(tm):
    j = pl.program_id(1)
    rid = lax.broadcasted_iota(jnp.int32, (tm + 2 * HALO, 1), 0)
    return jnp.logical_and(jnp.logical_or(rid >= HALO, j > 0),
                           jnp.logical_or(rid < tm + HALO, j < pl.num_programs(1) - 1))


def _dwconv_rows(ext, w_ref, c0, c1, taps, tm):
    n = ext.shape[0]
    acc = None
    for t in range(taps):
        off = t - taps // 2
        src = ext if off == 0 else pltpu.roll(ext, (-off) % n, 0)
        term = src * w_ref[t:t + 1, c0:c1]
        acc = term if acc is None else acc + term
    return acc[HALO:HALO + tm]


def _gdn_inproj_kernel(xp_ref, x_ref, xn_ref, sh_ref, sc_ref, w_ref, cw_ref, qkv_ref, z_ref, g_ref,
                       *, tm, heads, dk, tn):
    ext = jnp.concatenate([xp_ref[0], x_ref[0], xn_ref[0]], axis=0)
    hn = _rms_rows(ext) * (1.0 + sc_ref[0]) + sh_ref[0]
    hn = jnp.where(_halo_valid(tm), hn, 0.0).astype(BF16)
    hw = heads * dk
    for c0 in range(0, 3 * hw, tn):
        y = _dwconv_rows(_mm(hn, w_ref[:, c0:c0 + tn]), cw_ref, c0, c0 + tn, B_CONV, tm)
        y = y * _sigmoid(y)
        for g0 in range(0, tn, dk):
            yh = y[:, g0:g0 + dk]
            if c0 < 2 * hw:
                yh = yh * lax.rsqrt(jnp.sum(yh * yh, axis=-1, keepdims=True) + EPS)
            if c0 < hw:
                yh = yh * (dk ** -0.5)
            qkv_ref[0, :, c0 + g0:c0 + g0 + dk] = yh
    hm = hn[HALO:HALO + tm]
    for c0 in range(0, hw, tn):
        z_ref[0, :, c0:c0 + tn] = _mm(hm, w_ref[:, 3 * hw + c0:3 * hw + c0 + tn])
    g_ref[0] = _mm(hm, w_ref[:, 4 * hw:])


def _gdn_inproj(x, mod, w, conv_w):
    b, s, d = x.shape
    heads = B_HEADS
    n_gate = LANES
    hw = (w.shape[1] - n_gate) // 4
    tm = min(ROW_TILE, s)
    const = lambda i, j: (0, 0)
    row = lambda n: pl.BlockSpec((1, tm, n), lambda i, j: (i, j, 0))
    return pl.pallas_call(
        functools.partial(_gdn_inproj_kernel, tm=tm, heads=heads, dk=hw // heads, tn=512),
        out_shape=[jax.ShapeDtypeStruct((b, s, 3 * hw), F32),
                   jax.ShapeDtypeStruct((b, s, hw), F32),
                   jax.ShapeDtypeStruct((b, s, n_gate), F32)],
        grid=(b, s // tm),
        in_specs=_seq_halo_specs(tm, s, d) + [
            pl.BlockSpec((1, 1, d), lambda i, j: (i, 0, 0)),
            pl.BlockSpec((1, 1, d), lambda i, j: (i, 0, 1)),
            pl.BlockSpec(w.shape, const),
            pl.BlockSpec(conv_w.shape, const)],
        out_specs=[row(3 * hw), row(hw), row(n_gate)],
        compiler_params=_params("parallel", "parallel"),
        name="gdn_inproj",
    )(x, x, x, mod, mod, w, conv_w)


def _gdn_kernel(q_ref, k_ref, v_ref, gcol_ref, grow_ref, pcol_ref, prow_ref, *rest,
                reverse, chunk, group, heads, dk):
    add_ref = rest[0] if len(rest) == 6 else None
    o_ref, s_s, mq_s, r_s, gl_s = rest[-5:]

    @pl.when(pl.program_id(1) == 0)
    def _():
        s_s[...] = jnp.zeros_like(s_s)

    blk = q_ref.shape[1]
    cpg = group // chunk
    clog = chunk.bit_length() - 1
    row = lax.broadcasted_iota(jnp.int32, (group, group), 0)
    col = lax.broadcasted_iota(jnp.int32, (group, group), 1)

    def same_block(log2):
        return jnp.right_shift(row, log2) == jnp.right_shift(col, log2)

    same_chunk = same_block(clog)
    causal = jnp.logical_and(same_chunk, (col >= row) if reverse else (col <= row))
    causal_t = jnp.logical_and(same_chunk, (row >= col) if reverse else (row <= col))
    strict = jnp.logical_and(same_chunk, (col > row) if reverse else (col < row))
    eye = (row == col).astype(F32)
    base_log2 = 3
    causal_f, causal_tf = causal.astype(F32), causal_t.astype(F32)
    same_chunk_f, strict_f = same_chunk.astype(F32), strict.astype(F32)
    same_base_b = same_block(base_log2).astype(BF16)
    level_masks_b = [jnp.logical_and(same_block(lg + 1),
                                     jnp.logical_not(same_block(lg))).astype(BF16)
                     for lg in range(base_log2, clog)]
    chunk_of_row = jnp.right_shift(lax.broadcasted_iota(jnp.int32, (group, 1), 0), clog)

    raw_col = gcol_ref[0]
    raw_row = grow_ref[0]
    g_cols = -jnp.exp(pcol_ref[0:1, :]) * _softplus(raw_col[:, :heads] + pcol_ref[1:2, :])
    g_rows = -jnp.exp(prow_ref[:, 0:1]) * _softplus(raw_row[:heads, :] + prow_ref[:, 1:2])
    beta_cols = _sigmoid(raw_col[:, heads:])

    def par(fn, *lists):
        return [fn(*args) for args in zip(*lists)]

    def chunk_local(units):
        rows = [slice(gi * group, (gi + 1) * group) for _, gi in units]
        cols = [slice(h * dk, (h + 1) * dk) for h, _ in units]
        g_col = [g_cols[r, h:h + 1] for (h, _), r in zip(units, rows)]
        g_row = [g_rows[h:h + 1, r] for (h, _), r in zip(units, rows)]
        beta = [beta_cols[r, h:h + 1] for (h, _), r in zip(units, rows)]
        gc_col = par(lambda g: jnp.sum(causal_f * g, axis=1, keepdims=True), g_row)
        gc_row = par(lambda g: jnp.sum(causal_tf * g, axis=0, keepdims=True), g_col)
        tot_col = par(lambda g: jnp.sum(same_chunk_f * g, axis=1, keepdims=True), g_row)
        decay = par(lambda c, r: jnp.exp(jnp.where(causal, c - r, -jnp.inf)), gc_col, gc_row)
        q = [q_ref[0, r, c] for r, c in zip(rows, cols)]
        k = [k_ref[0, r, c] for r, c in zip(rows, cols)]
        v = [v_ref[0, r, c] for r, c in zip(rows, cols)]
        kb = par(lambda x: x.astype(BF16), k)
        kq = par(lambda kb_, q_: _mm_nt(jnp.concatenate([kb_, q_.astype(BF16)], axis=0), kb_),
                 kb, q)
        a = par(lambda kq_, d, b_: (kq_[:group] * d * strict_f * b_).astype(BF16), kq, decay, beta)
        n1 = par(lambda a_: a_ * same_base_b, a)
        n2 = par(lambda n: _mm_inv(n, n), n1)
        n4 = par(lambda n: _mm_inv(n, n), n2)
        x = par(lambda n, m: _mm_inv(eye - n, eye + m), n1, n2)
        x = par(lambda x_, n: _mm_inv(x_, eye + n), x, n4)
        for lm in level_masks_b:
            xl = par(lambda x_, a_: _mm_inv(x_, a_ * lm), x, a)
            x = par(lambda x_, xl_: x_ - _mm_inv(xl_, x_), x, xl)
        eg = par(jnp.exp, gc_col)
        wu = par(lambda x_, k_, b_, e, v_: _mm_inv(
            x_, jnp.concatenate([k_ * b_ * e, v_ * b_], axis=1)).astype(BF16), x, k, beta, eg, v)
        qo = par(lambda kq_, d, wu_: _mm((kq_[group:] * d).astype(BF16), wu_), kq, decay, wu)
        kd_by_chunk = par(lambda k_, t_, c_: jnp.concatenate(
            [jnp.where(chunk_of_row == c, k_ * jnp.exp(t_ - c_), 0.0) for c in range(cpg)],
            axis=1).astype(BF16), k, tot_col, gc_col)
        mr = par(_mm_tn, kd_by_chunk, wu)
        for i, (h, gi) in enumerate(units):
            o_ref[0, rows[i], cols[i]] = qo[i][:, dk:]
            q_prime = (q[i] * eg[i] - qo[i][:, :dk]).astype(BF16)
            for c in range(cpg):
                cg = gi * cpg + c
                mq_s[h, cg, 0:dk, :] = mr[i][c * dk:(c + 1) * dk, :dk].astype(BF16)
                mq_s[h, cg, dk:dk + chunk, :] = q_prime[c * chunk:(c + 1) * chunk]
                r_s[h, cg] = mr[i][c * dk:(c + 1) * dk, dk:]
                total = jnp.sum(g_row[i][:, c * chunk:(c + 1) * chunk], axis=1, keepdims=True)
                gl_s[h, cg] = jnp.broadcast_to(jnp.exp(total), (1, dk))

    all_units = [(h, gi) for h in range(heads) for gi in range(blk // group)]
    for u0 in range(0, len(all_units), GDN_UNITS):
        chunk_local(all_units[u0:u0 + GDN_UNITS])

    ncb = blk // chunk
    for ci in range(ncb):
        cc = (ncb - 1 - ci) if reverse else ci
        s_prev = [s_s[h] for h in range(heads)]
        p = [_mm(mq_s[h, cc], s_prev[h].astype(BF16)) for h in range(heads)]
        for h in range(heads):
            s_s[h] = gl_s[h, cc] * s_prev[h] - p[h][:dk] + r_s[h, cc]
            tile = (slice(cc * chunk, (cc + 1) * chunk), slice(h * dk, (h + 1) * dk))
            out = o_ref[0, tile[0], tile[1]] + p[h][dk:]
            if add_ref is not None:
                out = add_ref[0, tile[0], tile[1]] + out
            o_ref[0, tile[0], tile[1]] = out


def _gdn_scan(qkv, gcol, grow, pcol, prow, *, reverse, add=None):
    b, s, n = qkv.shape
    heads = B_HEADS
    hw = n // 3
    dk = hw // heads
    blk = min(SCAN_BLOCK, s)
    nb = s // blk
    ncb = blk // SCAN_CHUNK
    ng = gcol.shape[2]

    def seq(j):
        return (nb - 1 - j) if reverse else j

    out_spec = pl.BlockSpec((1, blk, hw), lambda i, j: (i, seq(j), 0))
    extra = [] if add is None else [add]
    return pl.pallas_call(
        functools.partial(_gdn_kernel, reverse=reverse, chunk=SCAN_CHUNK, group=GDN_GROUP,
                          heads=heads, dk=dk),
        out_shape=jax.ShapeDtypeStruct((b, s, hw), F32),
        grid=(b, nb),
        in_specs=[pl.BlockSpec((1, blk, hw), lambda i, j: (i, seq(j), 0)),
                  pl.BlockSpec((1, blk, hw), lambda i, j: (i, seq(j), 1)),
                  pl.BlockSpec((1, blk, hw), lambda i, j: (i, seq(j), 2)),
                  pl.BlockSpec((1, blk, ng), lambda i, j: (i, seq(j), 0)),
                  pl.BlockSpec((1, ng, blk), lambda i, j: (i, 0, seq(j))),
                  pl.BlockSpec((2, heads), lambda i, j: (0, 0)),
                  pl.BlockSpec((heads, 2), lambda i, j: (0, 0))] + [out_spec] * len(extra),
        out_specs=out_spec,
        scratch_shapes=[pltpu.VMEM((heads, dk, dk), F32),
                        pltpu.VMEM((heads, ncb, dk + SCAN_CHUNK, dk), BF16),
                        pltpu.VMEM((heads, ncb, dk, dk), F32),
                        pltpu.VMEM((heads, ncb, 1, dk), F32)],
        compiler_params=_params("parallel", "arbitrary"),
        name="gdn_scan_bwd" if reverse else "gdn_scan_fwd",
    )(qkv, qkv, qkv, gcol, grow, pcol, prow, *extra)


def _rel_bucket(rel):
    nb = REL_BUCKETS // 2
    exact = nb // 2
    n = jnp.abs(rel)
    large = exact + (jnp.log(jnp.maximum(n, 1).astype(jnp.float32) / exact)
                     / math.log(REL_MAX_DIST / exact) * (nb - exact)).astype(jnp.int32)
    large = jnp.minimum(large, nb - 1)
    return jnp.where(rel > 0, nb, 0) + jnp.where(n < exact, n, large)


def _bias_tiles(rel_bias, t):
    assert t >= REL_MAX_DIST
    n = 2 * t + 1
    m = jnp.arange(n, dtype=jnp.int32)
    rel_in_tile = jnp.where(m < t, m, m - n)
    d = jnp.arange(-2, 3, dtype=jnp.int32)
    rel = d[:, None] * t + rel_in_tile[None, :]
    vals = jnp.transpose(rel_bias.astype(F32)[_rel_bucket(rel)], (2, 0, 1)) * LOG2E
    h = vals.shape[0]
    tiled = jnp.tile(vals, (1, 1, t))[:, :, :t * (n - 1)].reshape(h, 5, t, n - 1)
    return tiled[:, :, :, :t]


def _attn_kernel(q_ref, k_ref, v_ref, bias_ref, lam_ref, ng_ref, o_ref,
                 m1_s, a1_s, m2_s, a2_s, vext_s, *, lambda_init, dh):
    qi = pl.program_id(2)
    ki = pl.program_id(3)
    t = bias_ref.shape[2]
    nq = q_ref.shape[1] // t
    tk, dv = v_ref.shape[1], v_ref.shape[2]
    r = tk // t
    maps = ((m1_s, a1_s), (m2_s, a2_s))

    @pl.when(ki == 0)
    def _():
        for m_s, a_s in maps:
            m_s[...] = jnp.full_like(m_s, -jnp.inf)
            a_s[...] = jnp.zeros_like(a_s)
        vext_s[:, dv:] = jnp.ones((tk, dv), BF16)

    vext_s[:, :dv] = v_ref[0]
    k_maps = [k_ref[0, :, i * dh:(i + 1) * dh] for i in range(2)]
    first = ki * r - qi * nq

    def update(with_bias):
        def logits(iq):
            rows = slice(iq * t, (iq + 1) * t)
            s = [_mm_nt(q_ref[0, rows, i * dh:(i + 1) * dh], k_maps[i]) for i in range(2)]
            if with_bias:
                bias = jnp.concatenate(
                    [bias_ref[0, jnp.clip(first - iq + j, -2, 2) + 2] for j in range(r)], axis=1)
                s = [s_ + bias for s_ in s]
            return s

        s_next = logits(0)
        for iq in range(nq):
            rows = slice(iq * t, (iq + 1) * t)
            s = s_next
            if iq + 1 < nq:
                s_next = logits(iq + 1)
            shift = 0.0 if with_bias else bias_ref[0, jnp.clip(first - iq, -2, 2) + 2, 0:1, 0:1]
            m_prev = [m_s[rows] for m_s, _ in maps]
            m_new = [jnp.maximum(m_prev[i], jnp.max(s[i], axis=1, keepdims=True) + shift)
                     for i in range(2)]
            p = [jnp.exp2(s[i] - (m_new[i] - shift)).astype(BF16) for i in range(2)]
            pv = [_mm(p[i], vext_s[...]) for i in range(2)]
            for i, (m_s, a_s) in enumerate(maps):
                a_s[rows] = jnp.exp2(m_prev[i] - m_new[i]) * a_s[rows] + pv[i]
                m_s[rows] = m_new[i]

    far = jnp.logical_or(first - (nq - 1) >= 2, first + (r - 1) <= -2)

    @pl.when(far)
    def _():
        update(False)

    @pl.when(jnp.logical_not(far))
    def _():
        update(True)

    @pl.when(ki == pl.num_programs(3) - 1)
    def _():
        lp = lam_ref[...]
        lam = (jnp.exp(jnp.sum(lp[0:1] * lp[1:2], axis=1, keepdims=True))
               - jnp.exp(jnp.sum(lp[2:3] * lp[3:4], axis=1, keepdims=True)) + lambda_init)
        a1, a2 = a1_s[...], a2_s[...]
        o = a1[:, :dv] / a1[:, dv:dv + 1] - lam * (a2[:, :dv] / a2[:, dv:dv + 1])
        o_ref[0] = _rms_rows(o) * ng_ref[...] * (1.0 - lambda_init)


def _diff_attn(proj, bias, lam_p, norm_g, lambda_init):
    b, s, n = proj.shape
    heads = C_HEADS
    dv = n // (3 * heads)
    t = bias.shape[2]
    tk = min(ATTN_KEYS, s)
    tq = min(ATTN_ROWS, s)
    return pl.pallas_call(
        functools.partial(_attn_kernel, lambda_init=lambda_init, dh=dv // 2),
        out_shape=jax.ShapeDtypeStruct((b, s, heads * dv), F32),
        grid=(b, heads, s // tq, s // tk),
        in_specs=[pl.BlockSpec((1, tq, dv), lambda i, h, qi, ki: (i, qi, h)),
                  pl.BlockSpec((1, tk, dv), lambda i, h, qi, ki: (i, ki, heads + h)),
                  pl.BlockSpec((1, tk, dv), lambda i, h, qi, ki: (i, ki, 2 * heads + h)),
                  pl.BlockSpec((1, 5, t, t), lambda i, h, qi, ki: (h, 0, 0, 0)),
                  pl.BlockSpec(lam_p.shape, lambda i, h, qi, ki: (0, 0)),
                  pl.BlockSpec((1, dv), lambda i, h, qi, ki: (0, 0))],
        out_specs=pl.BlockSpec((1, tq, dv), lambda i, h, qi, ki: (i, qi, h)),
        scratch_shapes=[pltpu.VMEM((tq, 1), F32), pltpu.VMEM((tq, 2 * dv), F32),
                        pltpu.VMEM((tq, 1), F32), pltpu.VMEM((tq, 2 * dv), F32),
                        pltpu.VMEM((tk, 2 * dv), BF16)],
        compiler_params=_params("parallel", "parallel", "parallel", "arbitrary"),
        name="diff_attn",
    )(proj, proj, proj, bias, lam_p, norm_g.reshape(1, dv))


def _outproj_kernel(*refs, mode, hd):
    if mode == "attn":
        a_ref, w_ref, x_ref, g_ref, o_ref = refs
        act = a_ref[0]
    else:
        a_ref, gate_ref, ng_ref, w_ref, x_ref, g_ref, o_ref = refs
        ssum = a_ref[0]
        d = ssum.shape[1]
        hs = jnp.concatenate([_rms_rows(ssum[:, c0:c0 + hd]) for c0 in range(0, d, hd)], axis=1)
        hs = hs * ng_ref[...]
        gate = gate_ref[0]
        if mode == "mlstm":
            act = _sigmoid(gate) * hs
        else:
            act = hs * (gate * _sigmoid(gate))
    o_ref[0] = x_ref[0] + g_ref[0] * _mm(act.astype(BF16), w_ref[...])


def _outproj(mode, x, mod, w, a, gate_src=None, gate_block=0, norm_g=None, hd=0):
    bsz, s, d = x.shape
    tm = min(OUTPROJ_ROWS, s)
    row = pl.BlockSpec((1, tm, d), lambda i, j: (i, j, 0))
    in_specs, args = [row], [a]
    if mode != "attn":
        in_specs += [pl.BlockSpec((1, tm, d), lambda i, j: (i, j, gate_block)),
                     pl.BlockSpec((1, d), lambda i, j: (0, 0))]
        args += [gate_src, norm_g.reshape(1, d)]
    in_specs += [pl.BlockSpec((d, d), lambda i, j: (0, 0)), row,
                 pl.BlockSpec((1, 1, d), lambda i, j: (i, 0, 2))]
    args += [w, x, mod]
    return pl.pallas_call(
        functools.partial(_outproj_kernel, mode=mode, hd=hd),
        out_shape=jax.ShapeDtypeStruct((bsz, s, d), F32),
        grid=(bsz, s // tm),
        in_specs=in_specs,
        out_specs=row,
        compiler_params=_params("parallel", "parallel"),
        name="outproj_" + mode,
    )(*args)


def _ffn_kernel(*refs, tm, dff, tn, final):
    if final:
        xp_ref, x_ref, xn_ref, sh_ref, sc_ref, g_ref, wu_ref, cw_ref, cb_ref, wd_ref, fg_ref, o_ref = refs
    else:
        xp_ref, x_ref, xn_ref, sh_ref, sc_ref, g_ref, wu_ref, cw_ref, cb_ref, wd_ref, o_ref = refs
    x = x_ref[0]
    ext = jnp.concatenate([xp_ref[0], x, xn_ref[0]], axis=0)
    hn = _rms_rows(ext) * (1.0 + sc_ref[0]) + sh_ref[0]
    hn = jnp.where(_halo_valid(tm), hn, 0.0).astype(BF16)
    acc = jnp.zeros_like(x)
    for c0 in range(0, dff, tn):
        c1 = min(c0 + tn, dff)
        ua = _dwconv_rows(_mm(hn, wu_ref[:, c0:c1]), cw_ref, c0, c1, FFN_CONV, tm)
        ug = _dwconv_rows(_mm(hn, wu_ref[:, dff + c0:dff + c1]), cw_ref, dff + c0, dff + c1,
                          FFN_CONV, tm)
        ua = ua + cb_ref[:, c0:c1]
        ug = ug + cb_ref[:, dff + c0:dff + c1]
        act = ua * (ug * _sigmoid(ug))
        acc = acc + _mm(act.astype(BF16), wd_ref[c0:c1, :])
    y = x + g_ref[0] * acc
    if final:
        y = _rms_rows(y) * fg_ref[...]
    o_ref[0] = y


def _conv_ffn(x, mod, w_up, conv_w, conv_b, w_down, final_g=None):
    b, s, d = x.shape
    dff = w_down.shape[0]
    tm = min(ROW_TILE, s)
    final = final_g is not None
    const = lambda i, j: (0, 0)
    in_specs = _seq_halo_specs(tm, s, d) + [
        pl.BlockSpec((1, 1, d), lambda i, j: (i, 0, 3)),
        pl.BlockSpec((1, 1, d), lambda i, j: (i, 0, 4)),
        pl.BlockSpec((1, 1, d), lambda i, j: (i, 0, 5)),
        pl.BlockSpec(w_up.shape, const),
        pl.BlockSpec(conv_w.shape, const),
        pl.BlockSpec((1, 2 * dff), const),
        pl.BlockSpec(w_down.shape, const)]
    args = [x, x, x, mod, mod, mod, w_up, conv_w, conv_b.reshape(1, 2 * dff), w_down]
    if final:
        in_specs.append(pl.BlockSpec((1, d), const))
        args.append(final_g.reshape(1, d))
    return pl.pallas_call(
        functools.partial(_ffn_kernel, tm=tm, dff=dff, tn=FFN_COLS, final=final),
        out_shape=jax.ShapeDtypeStruct((b, s, d), F32),
        grid=(b, s // tm),
        in_specs=in_specs,
        out_specs=pl.BlockSpec((1, tm, d), lambda i, j: (i, j, 0)),
        compiler_params=_params("parallel", "parallel"),
        name="conv_ffn",
    )(*args)


def _mlstm_layer(x, mod, w_in, gate_b, norm_g, w_out):
    heads = A_HEADS
    n_main = w_in.shape[1] - 4 * heads
    proj, gates = _inproj(x, mod, _split_w_in(w_in, n_main), n_main)
    hsum = None
    for direction in range(2):
        lo = 2 * heads * direction
        gcol, grow = _gate_layouts(gates, lo, 2 * heads, min(MLSTM_CHUNK, x.shape[1]))
        bias = gate_b[lo:lo + 2 * heads].astype(F32)
        hsum = _mlstm_scan(proj, gcol, grow, bias.reshape(1, -1), bias.reshape(-1, 1),
                           reverse=direction == 1, add=hsum)
    d = x.shape[2]
    return _outproj("mlstm", x, mod, w_out.astype(BF16), hsum, gate_src=proj,
                    gate_block=n_main // d - 1, norm_g=norm_g, hd=d // heads)


def _gdn_layer(x, mod, w_in, conv_w, a_log, dt_bias, norm_g, w_out):
    heads = B_HEADS
    n_main = w_in.shape[1] - 4 * heads
    qkv, z, gates = _gdn_inproj(x, mod, _split_w_in(w_in, n_main), conv_w)
    osum = None
    for direction in range(2):
        lo = 2 * heads * direction
        gcol = gates[:, :, lo:lo + 2 * heads]
        grow = jnp.swapaxes(gcol, 1, 2)
        p = jnp.stack([a_log[direction], dt_bias[direction]]).astype(F32)
        osum = _gdn_scan(qkv, gcol, grow, p, p.T, reverse=direction == 1, add=osum)
    d = x.shape[2]
    return _outproj("gdn", x, mod, w_out.astype(BF16), osum, gate_src=z,
                    gate_block=0, norm_g=jnp.tile(norm_g, heads), hd=d // heads)


def _attn_layer(x, mod, w_in, lam_p, norm_g, w_out, bias, lambda_init):
    n = w_in.shape[1]
    dh = n // (6 * C_HEADS)
    proj = _inproj(x, mod, w_in.astype(BF16), n, out_dtype=BF16, lead_cols=n // 3,
                   lead_scale=dh ** -0.5 * LOG2E)[0]
    o = _diff_attn(proj, bias, lam_p.astype(F32), norm_g, lambda_init)
    return _outproj("attn", x, mod, w_out.astype(BF16), o)


def _trunk(x, mods, bias_tiles, ada_w, ada_b, a_w_in, a_gate_b, a_norm_g, a_w_out, b_w_in, b_conv_w,
           b_a_log, b_dt_bias, b_norm_g, b_w_out, c_w_in, c_lambda, c_norm_g, c_w_out, rel_bias,
           ffn_w_up, ffn_conv_w, ffn_conv_b, ffn_w_down, final_g):
    depth = ada_w.shape[0]
    for i in range(depth):
        mod = mods[i][:, None, :]
        j = i // N_MIXERS
        kind = i % N_MIXERS
        if kind == 0:
            x = _mlstm_layer(x, mod, a_w_in[j], a_gate_b[j], a_norm_g[j], a_w_out[j])
        elif kind == 1:
            x = _gdn_layer(x, mod, b_w_in[j], b_conv_w[j], b_a_log[j], b_dt_bias[j], b_norm_g[j],
                           b_w_out[j])
        else:
            x = _attn_layer(x, mod, c_w_in[j], c_lambda[j], c_norm_g[j], c_w_out[j], bias_tiles,
                            0.8 - 0.6 * math.exp(-0.3 * i))
        x = _conv_ffn(x, mod, ffn_w_up[i].astype(BF16), ffn_conv_w[i], ffn_conv_b[i],
                      ffn_w_down[i].astype(BF16), final_g if i == depth - 1 else None)
    return x


def kernel(x_prompt, x_sample, c_prompt, c_sample, ada_w, ada_b, a_w_in, a_gate_b, a_norm_g, a_w_out, b_w_in, b_conv_w, b_a_log, b_dt_bias, b_norm_g, b_w_out, c_w_in, c_lambda, c_norm_g, c_w_out, rel_bias, ffn_w_up, ffn_conv_w, ffn_conv_b, ffn_w_down, final_g):
    weights = (ada_w, ada_b, a_w_in, a_gate_b, a_norm_g, a_w_out, b_w_in, b_conv_w, b_a_log,
               b_dt_bias, b_norm_g, b_w_out, c_w_in, c_lambda, c_norm_g, c_w_out, rel_bias,
               ffn_w_up, ffn_conv_w, ffn_conv_b, ffn_w_down, final_g)
    nb_p, nb_s = c_prompt.shape[0], c_sample.shape[0]
    c_all = jnp.concatenate([c_prompt, c_sample], axis=0)
    pad = (-c_all.shape[0]) % SUBLANES
    mods = _ada_mod(jnp.pad(c_all, ((0, pad), (0, 0))), ada_w, ada_b)
    outs, bias_tiles = [], {}
    for x, lo, n in ((x_prompt, 0, nb_p), (x_sample, nb_p, nb_s)):
        t = min(ATTN_TILE, x.shape[1])
        if t not in bias_tiles:
            bias_tiles[t] = _bias_tiles(rel_bias, t)
        outs.append(_trunk(x, mods[:, lo:lo + n], bias_tiles[t], *weights))
    return tuple(outs)
```
